```python
import math
import jax, jax.numpy as jnp
from jax import lax
import numpy as np

D_MODEL = 1024
BATCH = 8
SEQ = 4096
DEPTH = 1

CHUNK = 64
N_META = 16
SSD_HEADS = 16
SSD_HEAD_DIM = 64
SSD_INNER = SSD_HEADS * SSD_HEAD_DIM
SSD_GROUPS = 4
SSD_STATE = 128
SSD_CONV = 4
SSD_CONV_DIM = SSD_INNER + 2 * SSD_GROUPS * SSD_STATE
DA_HEADS = 8
DA_HEAD_DIM = 64
DA_V_DIM = 2 * DA_HEAD_DIM
DA_QK_WIDTH = DA_HEADS * 2 * DA_HEAD_DIM
DA_V_WIDTH = DA_HEADS * DA_V_DIM
Q_BLOCK = 128
N_BRANCH = 2
IN_SIZES = (SSD_INNER, SSD_CONV_DIM, SSD_HEADS, DA_QK_WIDTH, DA_QK_WIDTH, DA_V_WIDTH, N_BRANCH * D_MODEL)
IN_COLS = SSD_INNER + SSD_CONV_DIM + SSD_HEADS + 2 * DA_QK_WIDTH + DA_V_WIDTH + N_BRANCH * D_MODEL
N_EXPERTS = 32
TOP_K = 4
D_FF = D_MODEL
SWIGLU_LIMIT = 7.0
SWIGLU_ALPHA = 1.702
MOE_BLOCK = 128
DEEPNORM_ALPHA = (2.0 * DEPTH) ** 0.25
DEEPNORM_BETA = (8.0 * DEPTH) ** -0.25
LN_EPS = 1e-5
RMS_EPS = 1e-6

kernel_name = "hybrid_ssd_diffattn_moe_streaming"


def layer_norm(x, g, b):
    xf = x.astype(jnp.float32)
    mu = jnp.mean(xf, axis=-1, keepdims=True)
    var = jnp.mean(jnp.square(xf - mu), axis=-1, keepdims=True)
    return ((xf - mu) * lax.rsqrt(var + LN_EPS)).astype(x.dtype) * g + b


def split_points(sizes):
    pts, acc = [], 0
    for s in sizes[:-1]:
        acc += s
        pts.append(acc)
    return pts


def chunk_ids(pos):
    return jnp.where(pos < N_META, 0, 1 + (pos - N_META) // CHUNK)


def causal_dwconv(x, w, b):
    y = lax.conv_general_dilated(x, w[:, None, :].astype(x.dtype), window_strides=(1,),
                                 padding=((SSD_CONV - 1, 0),),
                                 dimension_numbers=("NWC", "WIO", "NWC"),
                                 feature_group_count=x.shape[-1])
    return y + b


def segsum_exp(a):
    T = a.shape[-1]
    cs = jnp.cumsum(a, axis=-1)
    diff = cs[..., :, None] - cs[..., None, :]
    mask = jnp.tril(jnp.ones((T, T), dtype=bool))
    return jnp.exp(jnp.where(mask, diff, -jnp.inf))


def ssd_mixer(z, xbc, dt_raw, conv_w, conv_b, dt_bias, a_log, d_skip, norm_w):
    f32 = jnp.float32
    Bsz, T, _ = z.shape
    xbc = jax.nn.silu(causal_dwconv(xbc, conv_w, conv_b))
    xs, Bm, Cm = jnp.split(xbc, [SSD_INNER, SSD_INNER + SSD_GROUPS * SSD_STATE], axis=-1)
    dt = jax.nn.softplus((dt_raw + dt_bias).astype(f32))
    A = -jnp.exp(a_log.astype(f32))
    front = (-N_META) % CHUNK
    back = (-(front + T)) % CHUNK
    nc = (front + T + back) // CHUNK
    r = SSD_HEADS // SSD_GROUPS

    def pad(t):
        return jnp.pad(t, ((0, 0), (front, back)) + ((0, 0),) * (t.ndim - 2))

    x_c = pad(xs.astype(f32)).reshape(Bsz, nc, CHUNK, SSD_GROUPS, r, SSD_HEAD_DIM)
    dt_c = pad(dt).reshape(Bsz, nc, CHUNK, SSD_GROUPS, r)
    B_c = pad(Bm.astype(f32)).reshape(Bsz, nc, CHUNK, SSD_GROUPS, SSD_STATE)
    C_c = pad(Cm.astype(f32)).reshape(Bsz, nc, CHUNK, SSD_GROUPS, SSD_STATE)
    xdt = x_c * dt_c[..., None]
    a_t = jnp.moveaxis(dt_c * A.reshape(SSD_GROUPS, r), 2, -1)
    a_cs = jnp.cumsum(a_t, axis=-1)
    Lmat = segsum_exp(a_t)
    cb = jnp.einsum("bclgn,bcsgn->bcgls", C_c, B_c)
    y_diag = jnp.einsum("bcgls,bcgrls,bcsgrp->bclgrp", cb, Lmat, xdt)
    decay_to_end = jnp.exp(a_cs[..., -1:] - a_cs)
    chunk_states = jnp.einsum("bclgn,bcgrl,bclgrp->bcgrpn", B_c, decay_to_end, xdt)
    chunk_decay = jnp.exp(a_cs[..., -1])

    def step(state, inp):
        s_c, d_c = inp
        return state * d_c[..., None, None] + s_c, state

    h0 = jnp.zeros((Bsz, SSD_GROUPS, r, SSD_HEAD_DIM, SSD_STATE), f32)
    _, prev = lax.scan(step, h0, (jnp.moveaxis(chunk_states, 1, 0), jnp.moveaxis(chunk_decay, 1, 0)))
    prev = jnp.moveaxis(prev, 0, 1)
    y_off = jnp.einsum("bclgn,bcgrpn,bcgrl->bclgrp", C_c, prev, jnp.exp(a_cs))
    y = (y_diag + y_off).reshape(Bsz, nc * CHUNK, SSD_INNER)[:, front:front + T]
    y = y + xs.astype(f32) * jnp.repeat(d_skip.astype(f32), SSD_HEAD_DIM)
    g = (y * jax.nn.silu(z.astype(f32))).reshape(Bsz, T, SSD_GROUPS, SSD_INNER // SSD_GROUPS)
    g = g * lax.rsqrt(jnp.mean(jnp.square(g), axis=-1, keepdims=True) + RMS_EPS)
    return (g.reshape(Bsz, T, SSD_INNER) * norm_w.astype(f32)).astype(z.dtype)


def diff_attention(q, k, v, lam_q1, lam_k1, lam_q2, lam_k2, subln_w, lambda_init):
    f32 = jnp.float32
    Bsz, T = q.shape[0], q.shape[1]
    lam = (jnp.exp(jnp.sum(lam_q1.astype(f32) * lam_k1.astype(f32)))
           - jnp.exp(jnp.sum(lam_q2.astype(f32) * lam_k2.astype(f32))) + lambda_init)
    n_blk = -(-T // Q_BLOCK)
    Tq = n_blk * Q_BLOCK
    qp = jnp.pad(q, ((0, 0), (0, Tq - T), (0, 0), (0, 0), (0, 0)))
    q_blocks = jnp.swapaxes(qp.reshape(Bsz, n_blk, Q_BLOCK, DA_HEADS, 2, DA_HEAD_DIM), 0, 1)
    q_pos = jnp.arange(Tq, dtype=jnp.int32).reshape(n_blk, Q_BLOCK)
    k_chunk = chunk_ids(jnp.arange(T, dtype=jnp.int32))
    scale = DA_HEAD_DIM ** -0.5

    def block(args):
        qb, qpos = args
        s = jnp.einsum("bqhjd,bkhjd->bhjqk", qb, k).astype(f32) * scale
        mask = k_chunk[None, :] <= chunk_ids(qpos)[:, None]
        p = jax.nn.softmax(jnp.where(mask, s, -jnp.inf), axis=-1)
        w = p[:, :, 0] - lam * p[:, :, 1]
        return jnp.einsum("bhqk,bkhe->bqhe", w.astype(v.dtype), v)

    o = lax.map(block, (q_blocks, q_pos))
    o = jnp.swapaxes(o, 0, 1).reshape(Bsz, Tq, DA_HEADS, DA_V_DIM)[:, :T].astype(f32)
    o = o * lax.rsqrt(jnp.mean(jnp.square(o), axis=-1, keepdims=True) + RMS_EPS) * subln_w.astype(f32)
    o = o * (1.0 - lambda_init)
    return o.reshape(Bsz, T, DA_V_WIDTH).astype(q.dtype)


def clamped_swiglu(hidden):
    gate, up = hidden[..., :D_FF], hidden[..., D_FF:]
    gate = jnp.minimum(gate, SWIGLU_LIMIT)
    up = jnp.clip(up, -SWIGLU_LIMIT, SWIGLU_LIMIT)
    return gate * jax.nn.sigmoid(SWIGLU_ALPHA * gate) * (up + 1.0)


def moe_ffn(x, w_router, b_router, w_gate_up, b_gate_up, w_down, b_down):
    Bsz, T, D = x.shape
    xf = x.reshape(-1, D)
    N = xf.shape[0]
    logits = (xf @ w_router + b_router).astype(jnp.float32)
    top_val, top_idx = lax.top_k(logits, TOP_K)
    gates = jax.nn.softmax(top_val, axis=-1)
    M = N * TOP_K
    e_flat = top_idx.reshape(-1).astype(jnp.int32)
    g_flat = gates.reshape(-1)
    tok_flat = jnp.arange(M, dtype=jnp.int32) // TOP_K
    order = jnp.argsort(e_flat)
    e_sorted = e_flat[order]
    counts = jnp.bincount(e_flat, length=N_EXPERTS)
    padded = (counts + MOE_BLOCK - 1) // MOE_BLOCK * MOE_BLOCK
    pad_end = jnp.cumsum(padded)
    pad_start = pad_end - padded
    grp_start = jnp.cumsum(counts) - counts
    dest = pad_start[e_sorted] + (jnp.arange(M, dtype=jnp.int32) - grp_start[e_sorted])
    n_blocks = -(-M // MOE_BLOCK) + N_EXPERTS
    S = n_blocks * MOE_BLOCK
    slot_tok = jnp.full((S,), N, jnp.int32).at[dest].set(tok_flat[order])
    slot_gate = jnp.zeros((S,), jnp.float32).at[dest].set(g_flat[order])
    blk_expert = jnp.minimum(
        jnp.searchsorted(pad_end, jnp.arange(n_blocks, dtype=pad_end.dtype) * MOE_BLOCK, side="right"),
        N_EXPERTS - 1).astype(jnp.int32)
    x_pad = jnp.concatenate([xf, jnp.zeros((1, D), xf.dtype)], axis=0)
    xs = x_pad[slot_tok].reshape(n_blocks, MOE_BLOCK, D)

    def expert_block(args):
        xb, e = args
        hidden = xb @ w_gate_up[e] + b_gate_up[e]
        return clamped_swiglu(hidden) @ w_down[e] + b_down[e]

    ys = lax.map(expert_block, (xs, blk_expert)).reshape(S, D)
    out = jax.ops.segment_sum(ys * slot_gate[:, None].astype(ys.dtype), slot_tok, num_segments=N + 1)[:N]
    return out.reshape(Bsz, T, D).astype(x.dtype)


def setup_inputs(seed: int = 0) -> dict:
    key = jax.random.key(seed)
    ks = jax.random.split(key, 40)
    f32 = jnp.float32
    nrm = lambda k, shape, s: jax.random.normal(k, shape, f32) * s
    dt = jnp.exp(jax.random.uniform(ks[8], (DEPTH, SSD_HEADS), f32) * (math.log(0.1) - math.log(0.001)) + math.log(0.001))
    return {
        "x": nrm(ks[0], (BATCH, SEQ, D_MODEL), 1.0),
        "meta_tokens": nrm(ks[1], (N_META, D_MODEL), 1.0),
        "ln_in_g": 1.0 + nrm(ks[2], (D_MODEL,), 0.02),
        "ln_in_b": nrm(ks[3], (D_MODEL,), 0.02),
        "w_in": nrm(ks[4], (DEPTH, D_MODEL, IN_COLS), D_MODEL ** -0.5),
        "b_gate": nrm(ks[5], (DEPTH, N_BRANCH * D_MODEL), 0.02),
        "conv_w": nrm(ks[6], (DEPTH, SSD_CONV, SSD_CONV_DIM), SSD_CONV ** -0.5),
        "conv_b": nrm(ks[7], (DEPTH, SSD_CONV_DIM), 0.02),
        "dt_bias": dt + jnp.log(-jnp.expm1(-dt)),
        "a_log": jnp.log(jax.random.uniform(ks[9], (DEPTH, SSD_HEADS), f32, 1.0, 16.0)),
        "d_skip": 1.0 + nrm(ks[10], (DEPTH, SSD_HEADS), 0.1),
        "ssd_norm_w": 1.0 + nrm(ks[11], (DEPTH, SSD_INNER), 0.02),
        "w_ssd_out": nrm(ks[12], (DEPTH, SSD_INNER, D_MODEL), SSD_INNER ** -0.5 * DEEPNORM_BETA),
        "lam_q1": nrm(ks[13], (DEPTH, DA_HEAD_DIM), 0.1),
        "lam_k1": nrm(ks[14], (DEPTH, DA_HEAD_DIM), 0.1),
        "lam_q2": nrm(ks[15], (DEPTH, DA_HEAD_DIM), 0.1),
        "lam_k2": nrm(ks[16], (DEPTH, DA_HEAD_DIM), 0.1),
        "subln_w": 1.0 + nrm(ks[17], (DEPTH, DA_V_DIM), 0.02),
        "w_da_out": nrm(ks[18], (DEPTH, DA_V_WIDTH, D_MODEL), DA_V_WIDTH ** -0.5 * DEEPNORM_BETA),
        "w_out": nrm(ks[19], (DEPTH, D_MODEL, D_MODEL), D_MODEL ** -0.5 * DEEPNORM_BETA),
        "ln1_g": 1.0 + nrm(ks[20], (DEPTH, D_MODEL), 0.02),
        "ln1_b": nrm(ks[21], (DEPTH, D_MODEL), 0.02),
        "w_router": nrm(ks[22], (DEPTH, D_MODEL, N_EXPERTS), D_MODEL ** -0.5),
        "b_router": nrm(ks[23], (DEPTH, N_EXPERTS), 0.01),
        "w_gate_up": nrm(ks[24], (DEPTH, N_EXPERTS, D_MODEL, 2 * D_FF), D_MODEL ** -0.5),
        "b_gate_up": nrm(ks[25], (DEPTH, N_EXPERTS, 2 * D_FF), 0.01),
        "w_down": nrm(ks[26], (DEPTH, N_EXPERTS, D_FF, D_MODEL), D_FF ** -0.5 * DEEPNORM_BETA),
        "b_down": nrm(ks[27], (DEPTH, N_EXPERTS, D_MODEL), 0.01),
        "ln2_g": 1.0 + nrm(ks[28], (DEPTH, D_MODEL), 0.02),
        "ln2_b": nrm(ks[29], (DEPTH, D_MODEL), 0.02),
    }


def reference(x, meta_tokens, ln_in_g, ln_in_b, w_in, b_gate, conv_w, conv_b, dt_bias, a_log,
              d_skip, ssd_norm_w, w_ssd_out, lam_q1, lam_k1, lam_q2, lam_k2, subln_w, w_da_out,
              w_out, ln1_g, ln1_b, w_router, b_router, w_gate_up, b_gate_up, w_down, b_down,
              ln2_g, ln2_b):
    Bsz = x.shape[0]
    meta = jnp.broadcast_to(meta_tokens[None].astype(x.dtype), (Bsz, N_META, D_MODEL))
    h = layer_norm(jnp.concatenate([meta, x], axis=1), ln_in_g, ln_in_b)
    T = h.shape[1]
    pts = split_points(IN_SIZES)
    for l in range(DEPTH):
        lambda_init = 0.8 - 0.6 * math.exp(-0.3 * l)
        proj = h @ w_in[l]
        z, xbc, dt_raw, q, k, v, gate_logits = jnp.split(proj, pts, axis=-1)
        y_ssd = ssd_mixer(z, xbc, dt_raw, conv_w[l], conv_b[l], dt_bias[l], a_log[l],
                          d_skip[l], ssd_norm_w[l]) @ w_ssd_out[l]
        y_da = diff_attention(q.reshape(Bsz, T, DA_HEADS, 2, DA_HEAD_DIM),
                              k.reshape(Bsz, T, DA_HEADS, 2, DA_HEAD_DIM),
                              v.reshape(Bsz, T, DA_HEADS, DA_V_DIM),
                              lam_q1[l], lam_k1[l], lam_q2[l], lam_k2[l], subln_w[l],
                              lambda_init) @ w_da_out[l]
        gates = jax.nn.sigmoid((gate_logits + b_gate[l]).astype(jnp.float32)).astype(h.dtype)
        merged = gates[..., :D_MODEL] * y_ssd + gates[..., D_MODEL:] * y_da
        mix = (merged @ w_out[l]).astype(h.dtype)
        h = layer_norm(DEEPNORM_ALPHA * h + mix, ln1_g[l], ln1_b[l])
        ffn = moe_ffn(h, w_router[l], b_router[l], w_gate_up[l], b_gate_up[l], w_down[l], b_down[l])
        h = layer_norm(DEEPNORM_ALPHA * h + ffn, ln2_g[l], ln2_b[l])
    return h[:, N_META:]
```

```python
import functools
import math

import jax
import jax.numpy as jnp
from jax import lax
from jax.experimental import pallas as pl
from jax.experimental.pallas import tpu as pltpu

F32 = jnp.float32
BF16 = jnp.bfloat16

D_MODEL = 1024
N_META = 16
SSD_HEADS = 16
SSD_HEAD_DIM = 64
SSD_INNER = 1024
SSD_GROUPS = 4
SSD_STATE = 128
SSD_CONV = 4
SSD_CONV_DIM = 2048
DA_HEADS = 8
DA_HEAD_DIM = 64
N_EXPERTS = 32
TOP_K = 4
D_FF = 1024
SWIGLU_LIMIT = 7.0
SWIGLU_ALPHA = 1.702
DEPTH = 1
DEEPNORM_ALPHA = (2.0 * DEPTH) ** 0.25
LN_EPS = 1e-5
RMS_EPS = 1e-6
LAMBDA_INIT = 0.8 - 0.6 * math.exp(-0.3 * 0)

LANES = 128
VMEM_LIMIT = 56 * 1024 * 1024

ROW_TILE = 512
ATT_TQ = 256
ATT_TK = 256
SSD_TILE = 256
SSD_CHUNK = 128
EXPERT_TILE = 512
MOVE_TILE = 256


def _cparams(sem):
    return pltpu.CompilerParams(dimension_semantics=sem, vmem_limit_bytes=VMEM_LIMIT)


def _const_spec(shape):
    nd = len(shape)
    return pl.BlockSpec(shape, lambda *a: (0,) * nd)


def _layer_norm(x, g, b):
    mu = jnp.mean(x, axis=-1, keepdims=True)
    xc = x - mu
    var = jnp.mean(xc * xc, axis=-1, keepdims=True)
    return xc * lax.rsqrt(var + LN_EPS) * g + b


def _sigmoid(x):
    return 1.0 / (1.0 + jnp.exp(-x))


def _dot(a, b):
    return jnp.dot(a, b, preferred_element_type=F32)


def _dot_nt(a, b):
    return lax.dot_general(a, b, (((1,), (1,)), ((), ())), preferred_element_type=F32)


def _qkv_kernel(x_ref, g_ref, b_ref, w_ref, q_ref, k_ref, v_ref):
    h = _layer_norm(x_ref[...], g_ref[...], b_ref[...]).astype(BF16)
    acc = _dot(h, w_ref[...])
    d = D_MODEL
    q_ref[...] = (acc[:, :d] * (DA_HEAD_DIM ** -0.5)).astype(BF16)
    k_ref[...] = acc[:, d:2 * d].astype(BF16)
    v_ref[...] = acc[:, 2 * d:].astype(BF16)


def _qkv_call(x2d, g, b, w_qkv, tm):
    n = x2d.shape[0]
    out = jax.ShapeDtypeStruct((n, D_MODEL), BF16)
    row = pl.BlockSpec((tm, D_MODEL), lambda i: (i, 0))
    return pl.pallas_call(
        _qkv_kernel,
        grid=(n // tm,),
        in_specs=[row, _const_spec((1, D_MODEL)), _const_spec((1, D_MODEL)),
                  _const_spec((D_MODEL, 3 * D_MODEL))],
        out_specs=[row, row, row],
        out_shape=[out, out, out],
        compiler_params=_cparams(("parallel",)),
    )(x2d, g, b, w_qkv)


def _attn_kernel(lam_ref, q_ref, k_ref, v_ref, km_ref, vm_ref, sw_ref, o_ref):
    tq, tk = ATT_TQ, ATT_TK
    i = pl.program_id(2)
    lane = lax.broadcasted_iota(jnp.int32, (1, LANES), 1)
    q = q_ref[...]
    zero = jnp.zeros_like(q)
    q0 = jnp.where(lane < DA_HEAD_DIM, q, zero)
    q1 = jnp.where(lane >= DA_HEAD_DIM, q, zero)

    def first(s, vt):
        m = jnp.max(s, axis=1, keepdims=True)
        p = jnp.exp(s - m)
        return m, jnp.sum(p, axis=1, keepdims=True), _dot(p.astype(BF16), vt)

    def update(state, s, vt):
        m, l, acc = state
        m_new = jnp.maximum(m, jnp.max(s, axis=1, keepdims=True))
        a = jnp.exp(m - m_new)
        p = jnp.exp(s - m_new)
        return (m_new, a * l + jnp.sum(p, axis=1, keepdims=True),
                a * acc + _dot(p.astype(BF16), vt))

    km = km_ref[...]
    vm = vm_ref[...]
    meta_ok = lax.broadcasted_iota(jnp.int32, (1, km.shape[0]), 1) < N_META
    st0 = first(jnp.where(meta_ok, _dot_nt(q0, km), -jnp.inf), vm)
    st1 = first(jnp.where(meta_ok, _dot_nt(q1, km), -jnp.inf), vm)

    def body(j, carry):
        s0, s1 = carry
        off = pl.multiple_of(j * tk, tk)
        kt = k_ref[pl.ds(off, tk), :]
        vt = v_ref[pl.ds(off, tk), :]
        return update(s0, _dot_nt(q0, kt), vt), update(s1, _dot_nt(q1, kt), vt)

    st0, st1 = lax.fori_loop(0, i, body, (st0, st1))

    off = pl.multiple_of(i * tk, tk)
    kt = k_ref[pl.ds(off, tk), :]
    vt = v_ref[pl.ds(off, tk), :]
    qc = lax.broadcasted_iota(jnp.int32, (tq, tk), 0) // 64
    kc = lax.broadcasted_iota(jnp.int32, (tq, tk), 1) // 64
    vis = kc <= qc
    m0, l0, a0 = update(st0, jnp.where(vis, _dot_nt(q0, kt), -jnp.inf), vt)
    m1, l1, a1 = update(st1, jnp.where(vis, _dot_nt(q1, kt), -jnp.inf), vt)

    o = a0 / l0 - lam_ref[0, 0] * (a1 / l1)
    o = o * lax.rsqrt(jnp.mean(o * o, axis=-1, keepdims=True) + RMS_EPS) * sw_ref[...]
    o_ref[...] = (o * (1.0 - LAMBDA_INIT)).astype(BF16)


def _attn_call(lam, q, k, v, km, vm, subln_w):
    bsz, s, _ = q.shape
    nq = s // ATT_TQ
    qspec = pl.BlockSpec((None, ATT_TQ, LANES), lambda b, h, i: (b, i, h))
    kvspec = pl.BlockSpec((None, s, LANES), lambda b, h, i: (b, 0, h))
    mspec = pl.BlockSpec((LANES, LANES), lambda b, h, i: (0, h))
    return pl.pallas_call(
        _attn_kernel,
        grid=(bsz, DA_HEADS, nq),
        in_specs=[pl.BlockSpec(memory_space=pltpu.SMEM), qspec, kvspec, kvspec, mspec, mspec,
                  _const_spec((1, LANES))],
        out_specs=qspec,
        out_shape=jax.ShapeDtypeStruct((bsz, s, D_MODEL), BF16),
        compiler_params=_cparams(("parallel", "parallel", "arbitrary")),
    )(lam, q, k, v, km, vm, subln_w)


def _split3(a):
    hi = a.astype(BF16)
    r = a - hi.astype(F32)
    mid = r.astype(BF16)
    lo = (r - mid.astype(F32)).astype(BF16)
    return hi, mid, lo


def _ssd_kernel(x_ref, meta_ref, g_ref, b_ref, wz_ref, wxbc_ref, wdt_ref, cw_ref, cb_ref,
                dtb_ref, alog_ref, dskip_ref, nw_ref, o_ref, state_ref, cbuf_ref, y_ref):
    lt, lc = SSD_TILE, SSD_CHUNK
    t = pl.program_id(1)
    is_meta = t == 0

    @pl.when(is_meta)
    def _():
        state_ref[...] = jnp.zeros_like(state_ref)
        cbuf_ref[0:8, :] = jnp.zeros((8, SSD_CONV_DIM), F32)

    row = lax.broadcasted_iota(jnp.int32, (lt, 1), 0)
    valid = jnp.logical_or(jnp.logical_not(is_meta), row >= lt - N_META)
    x = jnp.where(is_meta, meta_ref[...], x_ref[...])
    h = _layer_norm(x, g_ref[...], b_ref[...]).astype(BF16)
    z = _dot(h, wz_ref[...])
    xbc = jnp.where(valid, _dot(h, wxbc_ref[...]), 0.0)
    dtr = _dot(h, wdt_ref[...])

    cbuf_ref[8:8 + lt, :] = xbc
    conv = (cw_ref[0:1, :] * cbuf_ref[5:5 + lt, :] + cw_ref[1:2, :] * cbuf_ref[6:6 + lt, :]
            + cw_ref[2:3, :] * cbuf_ref[7:7 + lt, :] + cw_ref[3:4, :] * xbc + cb_ref[...])
    cbuf_ref[0:8, :] = cbuf_ref[lt:lt + 8, :]
    act = jnp.where(valid, conv * _sigmoid(conv), 0.0)
    xs = act[:, :SSD_INNER]
    bm = act[:, SSD_INNER:SSD_INNER + SSD_GROUPS * SSD_STATE]
    cm = act[:, SSD_INNER + SSD_GROUPS * SSD_STATE:].astype(BF16)
    bm_t = bm.T.astype(BF16)

    dtv = dtr + dtb_ref[...]
    dt = jnp.maximum(dtv, 0.0) + jnp.log1p(jnp.exp(-jnp.abs(dtv)))
    dt = jnp.where(valid, dt, 0.0)
    a = dt * (-jnp.exp(alog_ref[...]))

    lane = lax.broadcasted_iota(jnp.int32, (1, LANES), 1)
    left = lane < SSD_HEAD_DIM
    tri_r = lax.broadcasted_iota(jnp.int32, (lc, lc), 0)
    tri_c = lax.broadcasted_iota(jnp.int32, (lc, lc), 1)
    causal = tri_c <= tri_r
    tri = jnp.where(causal, 1.0, 0.0).astype(BF16)

    for c in range(lt // lc):
        rs = slice(c * lc, (c + 1) * lc)
        hi, mid, lo = _split3(a[rs])
        acs = _dot(tri, hi) + _dot(tri, mid) + _dot(tri, lo)
        acs_t = acs.T
        dt_c = dt[rs]
        for g in range(SSD_GROUPS):
            c_g = cm[rs, g * SSD_STATE:(g + 1) * SSD_STATE]
            bt_g = bm_t[g * SSD_STATE:(g + 1) * SSD_STATE, rs]
            cb = _dot(c_g, bt_g)
            for jj in range(2):
                j = 2 * g + jj
                h0, h1 = 2 * j, 2 * j + 1
                col0, col1 = acs[:, h0:h0 + 1], acs[:, h1:h1 + 1]
                row0, row1 = acs_t[h0:h0 + 1, :], acs_t[h1:h1 + 1, :]
                l0 = jnp.exp(jnp.where(causal, col0 - row0, -jnp.inf))
                l1 = jnp.exp(jnp.where(causal, col1 - row1, -jnp.inf))
                mm = jnp.concatenate([(cb * l0).astype(BF16), (cb * l1).astype(BF16)], axis=0)
                xp = xs[rs, j * LANES:(j + 1) * LANES]
                xdt = xp * jnp.where(left, dt_c[:, h0:h0 + 1], dt_c[:, h1:h1 + 1])
                yy = _dot(mm, xdt.astype(BF16))
                y_diag = jnp.where(left, yy[:lc], yy[lc:])
                acs_p = jnp.where(left, col0, col1)
                st = state_ref[j]
                y_off = _dot(c_g, st.astype(BF16)) * jnp.exp(acs_p)
                last_p = jnp.where(left, acs[lc - 1:lc, h0:h0 + 1], acs[lc - 1:lc, h1:h1 + 1])
                xd = (xdt * jnp.exp(last_p - acs_p)).astype(BF16)
                state_ref[j] = st * jnp.exp(last_p) + _dot(bt_g, xd)
                y_ref[rs, j * LANES:(j + 1) * LANES] = (
                    y_diag + y_off + xp * dskip_ref[:, j * LANES:(j + 1) * LANES])

    gy = y_ref[...] * (z * _sigmoid(z))
    gw = SSD_INNER // SSD_GROUPS
    outs = []
    for g in range(SSD_GROUPS):
        gg = gy[:, g * gw:(g + 1) * gw]
        outs.append(gg * lax.rsqrt(jnp.mean(gg * gg, axis=-1, keepdims=True) + RMS_EPS))
    o_ref[...] = (jnp.concatenate(outs, axis=1) * nw_ref[...]).astype(BF16)


def _ssd_call(x3, meta_tile, g, b, wz, wxbc, wdt, cw, cb, dtb, alog, dskip, nw):
    bsz, s, _ = x3.shape
    lt = SSD_TILE
    nt = s // lt
    xspec = pl.BlockSpec((None, lt, D_MODEL), lambda bb, t: (bb, jnp.maximum(t - 1, 0), 0))
    return pl.pallas_call(
        _ssd_kernel,
        grid=(bsz, nt + 1),
        in_specs=[xspec, _const_spec((lt, D_MODEL)), _const_spec((1, D_MODEL)),
                  _const_spec((1, D_MODEL)), _const_spec((D_MODEL, SSD_INNER)),
                  _const_spec((D_MODEL, SSD_CONV_DIM)), _const_spec((D_MODEL, LANES)),
                  _const_spec((SSD_CONV, SSD_CONV_DIM)), _const_spec((1, SSD_CONV_DIM)),
                  _const_spec((1, LANES)), _const_spec((1, LANES)), _const_spec((1, SSD_INNER)),
                  _const_spec((1, SSD_INNER))],
        out_specs=xspec,
        out_shape=jax.ShapeDtypeStruct((bsz, s, SSD_INNER), BF16),
        scratch_shapes=[pltpu.VMEM((SSD_HEADS // 2, SSD_STATE, LANES), F32),
                        pltpu.VMEM((lt + 8, SSD_CONV_DIM), F32),
                        pltpu.VMEM((lt, SSD_INNER), F32)],
        compiler_params=_cparams(("parallel", "arbitrary")),
    )(x3, meta_tile, g, b, wz, wxbc, wdt, cw, cb, dtb, alog, dskip, nw)


def _merge_kernel(x_ref, ssd_ref, da_ref, g0_ref, b0_ref, wg_ref, bg_ref, wso_ref, wdo_ref, wo_ref,
                  g1_ref, b1_ref, wr_ref, br_ref,
                  h1_ref, hp_ref, ri_ref, rg_ref, cnt_ref, carry_ref):
    tm = x_ref.shape[0]
    d = D_MODEL

    @pl.when(pl.program_id(0) == 0)
    def _():
        carry_ref[...] = jnp.zeros_like(carry_ref)

    h = _layer_norm(x_ref[...], g0_ref[...], b0_ref[...])
    gates = _sigmoid(_dot(h.astype(BF16), wg_ref[...]) + bg_ref[...])
    y_ssd = _dot(ssd_ref[...], wso_ref[...])
    y_da = _dot(da_ref[...], wdo_ref[...])
    merged = gates[:, :d] * y_ssd + gates[:, d:] * y_da
    mix = _dot(merged.astype(BF16), wo_ref[...])
    h1 = _layer_norm(DEEPNORM_ALPHA * h + mix, g1_ref[...], b1_ref[...])
    h1_ref[...] = h1

    hb = h1.astype(BF16)
    bits = pltpu.bitcast(hb.astype(F32), jnp.uint32)
    hp_ref[...] = bits[:, :d // 2] | (bits[:, d // 2:] >> 16)

    lane = lax.broadcasted_iota(jnp.int32, (tm, LANES), 1)
    logits = _dot(hb, wr_ref[...]) + br_ref[...]
    logits = jnp.where(lane < N_EXPERTS, logits, -jnp.inf)
    vals, sels = [], []
    ri = jnp.zeros((tm, LANES), jnp.int32)
    for k in range(TOP_K):
        mx = jnp.max(logits, axis=1, keepdims=True)
        idx = jnp.min(jnp.where(logits == mx, lane, LANES), axis=1, keepdims=True)
        sel = lane == idx
        logits = jnp.where(sel, -jnp.inf, logits)
        vals.append(mx)
        sels.append(sel)
        ri = jnp.where(lane == k, idx, ri)
    exps = [jnp.exp(v - vals[0]) for v in vals]
    den = exps[0] + exps[1] + exps[2] + exps[3]
    rg = jnp.zeros((tm, LANES), F32)
    for k in range(TOP_K):
        rg = jnp.where(lane == k, exps[k] / den, rg)
    rg_ref[...] = rg

    onehot = jnp.where(sels[0] | sels[1] | sels[2] | sels[3], 1.0, 0.0)
    r_i = lax.broadcasted_iota(jnp.int32, (tm, tm), 0)
    c_i = lax.broadcasted_iota(jnp.int32, (tm, tm), 1)
    strict = jnp.where(c_i < r_i, 1.0, 0.0).astype(BF16)
    before = _dot(strict, onehot.astype(BF16)) + carry_ref[0:1, :]
    for k in range(TOP_K):
        rk = jnp.sum(jnp.where(sels[k], before, 0.0), axis=1, keepdims=True)
        ri = jnp.where(lane == TOP_K + k, rk.astype(jnp.int32), ri)
    ri_ref[...] = ri
    total = carry_ref[...] + jnp.sum(onehot, axis=0, keepdims=True)
    carry_ref[...] = total
    cnt_ref[...] = total


def _merge_call(x2d, ssd_n, da_n, g0, b0, wg, bg, wso, wdo, wo, g1, b1, wr, br, tm):
    n = x2d.shape[0]
    d = D_MODEL
    rowf = pl.BlockSpec((tm, d), lambda i: (i, 0))
    rowl = pl.BlockSpec((tm, LANES), lambda i: (i, 0))
    return pl.pallas_call(
        _merge_kernel,
        grid=(n // tm,),
        in_specs=[rowf, rowf, rowf, _const_spec((1, d)), _const_spec((1, d)),
                  _const_spec((d, 2 * d)), _const_spec((1, 2 * d)), _const_spec((d, d)),
                  _const_spec((d, d)), _const_spec((d, d)), _const_spec((1, d)), _const_spec((1, d)),
                  _const_spec((d, LANES)), _const_spec((1, LANES))],
        out_specs=[rowf, pl.BlockSpec((tm, d // 2), lambda i: (i, 0)), rowl, rowl,
                   _const_spec((8, LANES))],
        out_shape=[jax.ShapeDtypeStruct((n, d), F32), jax.ShapeDtypeStruct((n, d // 2), jnp.uint32),
                   jax.ShapeDtypeStruct((n, LANES), jnp.int32), jax.ShapeDtypeStruct((n, LANES), F32),
                   jax.ShapeDtypeStruct((8, LANES), F32)],
        scratch_shapes=[pltpu.VMEM((8, LANES), F32)],
        compiler_params=_cparams(("arbitrary",)),
    )(x2d, ssd_n, da_n, g0, b0, wg, bg, wso, wdo, wo, g1, b1, wr, br)


def _row_copy(src, dst, sem):
    return pltpu.make_async_copy(src, dst, sem)


def _dispatch_kernel(dest_ref, hp_ref, xs_ref, sem):
    tm = hp_ref.shape[0]

    def issue(r, carry):
        for k in range(TOP_K):
            d = dest_ref[0, r * TOP_K + k]
            _row_copy(hp_ref.at[pl.ds(r, 1), :], xs_ref.at[pl.ds(d, 1), :], sem).start()
        return carry

    lax.fori_loop(0, tm, issue, 0)

    def drain(r, carry):
        for k in range(TOP_K):
            d = dest_ref[0, r * TOP_K + k]
            _row_copy(hp_ref.at[pl.ds(r, 1), :], xs_ref.at[pl.ds(d, 1), :], sem).wait()
        return carry

    lax.fori_loop(0, tm, drain, 0)


def _dispatch_call(dest3, hp, tm):
    n, w = hp.shape
    return pl.pallas_call(
        _dispatch_kernel,
        grid=(n // tm,),
        in_specs=[pl.BlockSpec((None, 1, tm * TOP_K), lambda i: (i, 0, 0), memory_space=pltpu.SMEM),
                  pl.BlockSpec((tm, w), lambda i: (i, 0))],
        out_specs=pl.BlockSpec(memory_space=pl.ANY),
        out_shape=jax.ShapeDtypeStruct((n * TOP_K, w), jnp.uint32),
        scratch_shapes=[pltpu.SemaphoreType.DMA(())],
        compiler_params=_cparams(("arbitrary",)),
    )(dest3, hp)


def _expert_kernel(tile_ref, exp_ref, lo_ref, hi_ref, xs_ref, wgu_ref, bgu_ref, wd_ref, bd_ref, ys_ref):
    i = pl.program_id(0)
    lo = lo_ref[i]
    hi = hi_ref[i]
    tm = xs_ref.shape[0]

    @pl.when(lo < hi)
    def _():
        w = xs_ref[...]
        x_hi = pltpu.bitcast(w & jnp.uint32(0xFFFF0000), F32).astype(BF16)
        x_lo = pltpu.bitcast(w << 16, F32).astype(BF16)
        x = jnp.concatenate([x_hi, x_lo], axis=1)
        hid = _dot(x, wgu_ref[...]) + bgu_ref[...]
        gate = jnp.minimum(hid[:, :D_FF], SWIGLU_LIMIT)
        up = jnp.clip(hid[:, D_FF:], -SWIGLU_LIMIT, SWIGLU_LIMIT)
        act = gate * _sigmoid(SWIGLU_ALPHA * gate) * (up + 1.0)
        y = _dot(act.astype(BF16), wd_ref[...]) + bd_ref[...]
        row = lax.broadcasted_iota(jnp.int32, (tm, 1), 0)
        mine = jnp.logical_and(row >= lo, row < hi)

        @pl.when(lo == 0)
        def _():
            ys_ref[...] = jnp.where(mine, y, 0.0)

        @pl.when(lo > 0)
        def _():
            ys_ref[...] = jnp.where(mine, y, ys_ref[...])


def _expert_call(item_tile, item_exp, item_lo, item_hi, xs, wgu, bgu, wd, bd):
    m, w = xs.shape
    tm = EXPERT_TILE
    n_items = item_tile.shape[0]
    grid_spec = pltpu.PrefetchScalarGridSpec(
        num_scalar_prefetch=4,
        grid=(n_items,),
        in_specs=[pl.BlockSpec((tm, w), lambda i, t, e, lo, hi: (t[i], 0)),
                  pl.BlockSpec((None, D_MODEL, 2 * D_FF), lambda i, t, e, lo, hi: (e[i], 0, 0)),
                  pl.BlockSpec((None, 1, 2 * D_FF), lambda i, t, e, lo, hi: (e[i], 0, 0)),
                  pl.BlockSpec((None, D_FF, D_MODEL), lambda i, t, e, lo, hi: (e[i], 0, 0)),
                  pl.BlockSpec((None, 1, D_MODEL), lambda i, t, e, lo, hi: (e[i], 0, 0))],
        out_specs=pl.BlockSpec((tm, D_MODEL), lambda i, t, e, lo, hi: (t[i], 0)),
    )
    return pl.pallas_call(
        _expert_kernel,
        grid_spec=grid_spec,
        out_shape=jax.ShapeDtypeStruct((m, D_MODEL), F32),
        compiler_params=_cparams(("arbitrary",)),
    )(item_tile, item_exp, item_lo, item_hi, xs, wgu, bgu, wd, bd)


def _combine_kernel(dest_ref, ys_ref, h1_ref, rg_ref, g2_ref, b2_ref, o_ref, buf_ref, sem):
    tm = h1_ref.shape[0]

    def issue(r, carry):
        for k in range(TOP_K):
            d = dest_ref[0, r * TOP_K + k]
            _row_copy(ys_ref.at[pl.ds(d, 1), :], buf_ref.at[k, pl.ds(r, 1), :], sem).start()
        return carry

    lax.fori_loop(0, tm, issue, 0)

    def drain(r, carry):
        for k in range(TOP_K):
            d = dest_ref[0, r * TOP_K + k]
            _row_copy(ys_ref.at[pl.ds(d, 1), :], buf_ref.at[k, pl.ds(r, 1), :], sem).wait()
        return carry

    lax.fori_loop(0, tm, drain, 0)

    rg = rg_ref[...]
    ffn = rg[:, 0:1] * buf_ref[0]
    for k in range(1, TOP_K):
        ffn = ffn + rg[:, k:k + 1] * buf_ref[k]
    o_ref[...] = _layer_norm(DEEPNORM_ALPHA * h1_ref[...] + ffn, g2_ref[...], b2_ref[...])


def _combine_call(dest3, ys, h1, rg, g2, b2, tm):
    n, d = h1.shape
    rowf = pl.BlockSpec((tm, d), lambda i: (i, 0))
    return pl.pallas_call(
        _combine_kernel,
        grid=(n // tm,),
        in_specs=[pl.BlockSpec((None, 1, tm * TOP_K), lambda i: (i, 0, 0), memory_space=pltpu.SMEM),
                  pl.BlockSpec(memory_space=pl.ANY), rowf,
                  pl.BlockSpec((tm, LANES), lambda i: (i, 0)), _const_spec((1, d)), _const_spec((1, d))],
        out_specs=rowf,
        out_shape=jax.ShapeDtypeStruct((n, d), F32),
        scratch_shapes=[pltpu.VMEM((TOP_K, tm, d), F32), pltpu.SemaphoreType.DMA(())],
        compiler_params=_cparams(("arbitrary",)),
    )(dest3, ys, h1, rg, g2, b2)


def _work_items(counts, m):
    tm = EXPERT_TILE
    n_tiles = m // tm
    max_items = n_tiles + N_EXPERTS - 1
    grp_end = jnp.cumsum(counts)
    grp_start = grp_end - counts
    first_tile = grp_start // tm
    last_tile = (grp_end - 1) // tm
    n_e = jnp.where(counts > 0, last_tile - first_tile + 1, 0)
    item_end = jnp.cumsum(n_e)
    item_start = item_end - n_e
    total = item_end[-1]
    i = jnp.arange(max_items, dtype=jnp.int32)
    valid = i < total
    ic = jnp.minimum(i, total - 1)
    e = jnp.minimum(jnp.searchsorted(item_end, ic, side="right"), N_EXPERTS - 1).astype(jnp.int32)
    tile = first_tile[e] + (ic - item_start[e])
    lo = jnp.maximum(grp_start[e], tile * tm) - tile * tm
    hi = jnp.minimum(grp_end[e], (tile + 1) * tm) - tile * tm
    zero = jnp.zeros_like(lo)
    return (tile.astype(jnp.int32), e, jnp.where(valid, lo, zero).astype(jnp.int32),
            jnp.where(valid, hi, zero).astype(jnp.int32), grp_start)


def kernel(x, meta_tokens, ln_in_g, ln_in_b, w_in, b_gate, conv_w, conv_b, dt_bias, a_log, d_skip, ssd_norm_w, w_ssd_out, lam_q1, lam_k1, lam_q2, lam_k2, subln_w, w_da_out, w_out, ln1_g, ln1_b, w_router, b_router, w_gate_up, b_gate_up, w_down, b_down, ln2_g, ln2_b):
    bsz, s, d = x.shape
    n = bsz * s
    l = 0
    row = lambda v: v.reshape(1, -1).astype(F32)

    w = w_in[l]
    c0 = SSD_INNER
    c1 = c0 + SSD_CONV_DIM
    c2 = c1 + SSD_HEADS
    c3 = c2 + 3 * D_MODEL
    w_z = w[:, :c0].astype(BF16)
    w_xbc = w[:, c0:c1].astype(BF16)
    w_dt = jnp.pad(w[:, c1:c2], ((0, 0), (0, LANES - SSD_HEADS))).astype(BF16)
    w_qkv = w[:, c2:c3].astype(BF16)
    w_g = w[:, c3:].astype(BF16)
    g0, b0 = row(ln_in_g), row(ln_in_b)

    x2d = x.reshape(n, d)
    q, k, v = _qkv_call(x2d, g0, b0, w_qkv, ROW_TILE)
    _, km, vm = _qkv_call(meta_tokens.astype(F32), g0, b0, w_qkv, N_META)
    km = jnp.pad(km, ((0, LANES - N_META), (0, 0)))
    vm = jnp.pad(vm, ((0, LANES - N_META), (0, 0)))

    lam = (jnp.exp(jnp.sum(lam_q1[l].astype(F32) * lam_k1[l].astype(F32)))
           - jnp.exp(jnp.sum(lam_q2[l].astype(F32) * lam_k2[l].astype(F32))) + LAMBDA_INIT)
    da_n = _attn_call(lam.reshape(1, 1), q.reshape(bsz, s, d), k.reshape(bsz, s, d),
                      v.reshape(bsz, s, d), km, vm, row(subln_w[l]))

    meta_tile = jnp.pad(meta_tokens.astype(F32), ((SSD_TILE - N_META, 0), (0, 0)))
    pad_h = lambda vec: jnp.pad(row(vec), ((0, 0), (0, LANES - SSD_HEADS)))
    ssd_n = _ssd_call(x, meta_tile, g0, b0, w_z, w_xbc, w_dt, conv_w[l].astype(F32), row(conv_b[l]),
                      pad_h(dt_bias[l]), pad_h(a_log[l]),
                      row(jnp.repeat(d_skip[l].astype(F32), SSD_HEAD_DIM)), row(ssd_norm_w[l]))

    w_r = jnp.pad(w_router[l], ((0, 0), (0, LANES - N_EXPERTS))).astype(BF16)
    b_r = jnp.pad(row(b_router[l]), ((0, 0), (0, LANES - N_EXPERTS)))
    h1, hp, ri, rg, cnt = _merge_call(
        x2d, ssd_n.reshape(n, d), da_n.reshape(n, d), g0, b0, w_g, row(b_gate[l]),
        w_ssd_out[l].astype(BF16), w_da_out[l].astype(BF16), w_out[l].astype(BF16),
        row(ln1_g[l]), row(ln1_b[l]), w_r, b_r, ROW_TILE)

    counts = cnt[0, :N_EXPERTS].astype(jnp.int32)
    m = n * TOP_K
    item_tile, item_exp, item_lo, item_hi, grp_start = _work_items(counts, m)
    dest = grp_start[ri[:, :TOP_K]] + ri[:, TOP_K:2 * TOP_K]
    dest3 = dest.astype(jnp.int32).reshape(n // MOVE_TILE, 1, MOVE_TILE * TOP_K)

    xs = _dispatch_call(dest3, hp, MOVE_TILE)
    ys = _expert_call(item_tile, item_exp, item_lo, item_hi, xs,
                      w_gate_up[l].astype(BF16), b_gate_up[l].reshape(N_EXPERTS, 1, -1).astype(F32),
                      w_down[l].astype(BF16), b_down[l].reshape(N_EXPERTS, 1, -1).astype(F32))
    out = _combine_call(dest3, ys, h1, rg, row(ln2_g[l]), row(ln2_b[l]), MOVE_TILE)
    return out.reshape(bsz, s, d)
```

```python
import functools
import math

import jax
import jax.numpy as jnp
from jax import lax
from jax.experimental import pallas as pl
from jax.experimental.pallas import tpu as pltpu

F32 = jnp.float32
BF16 = jnp.bfloat16

D_MODEL = 1024
N_META = 16
SSD_HEADS = 16
SSD_HEAD_DIM = 64
SSD_INNER = 1024
SSD_GROUPS = 4
SSD_STATE = 128
SSD_CONV = 4
SSD_CONV_DIM = 2048
DA_HEADS = 8
DA_HEAD_DIM = 64
N_EXPERTS = 32
TOP_K = 4
D_FF = 1024
SWIGLU_LIMIT = 7.0
SWIGLU_ALPHA = 1.702
DEPTH = 1
DEEPNORM_ALPHA = (2.0 * DEPTH) ** 0.25
LN_EPS = 1e-5
RMS_EPS = 1e-6
LAMBDA_INIT = 0.8 - 0.6 * math.exp(-0.3 * 0)

LANES = 128
VMEM_LIMIT = 56 * 1024 * 1024

ROW_TILE = 512
ATT_TQ = 512
ATT_TK = 512
SSD_TILE = 256
SSD_CHUNK = 128
EXPERT_TILE = 512
MOVE_TILE = 256


def _cparams(sem):
    return pltpu.CompilerParams(dimension_semantics=sem, vmem_limit_bytes=VMEM_LIMIT)


def _const_spec(shape):
    nd = len(shape)
    return pl.BlockSpec(shape, lambda *a: (0,) * nd)


def _layer_norm(x, g, b):
    mu = jnp.mean(x, axis=-1, keepdims=True)
    xc = x - mu
    var = jnp.mean(xc * xc, axis=-1, keepdims=True)
    return xc * lax.rsqrt(var + LN_EPS) * g + b


def _sigmoid(x):
    return 1.0 / (1.0 + jnp.exp(-x))


def _dot(a, b):
    return jnp.dot(a, b, preferred_element_type=F32)


def _dot_nt(a, b):
    return lax.dot_general(a, b, (((1,), (1,)), ((), ())), preferred_element_type=F32)


def _qkv_kernel(x_ref, g_ref, b_ref, w_ref, q_ref, k_ref, v_ref):
    h = _layer_norm(x_ref[...], g_ref[...], b_ref[...]).astype(BF16)
    acc = _dot(h, w_ref[...])
    d = D_MODEL
    q_ref[...] = (acc[:, :d] * (DA_HEAD_DIM ** -0.5)).astype(BF16)
    k_ref[...] = acc[:, d:2 * d].astype(BF16)
    v_ref[...] = acc[:, 2 * d:].astype(BF16)


def _qkv_call(x2d, g, b, w_qkv, tm):
    n = x2d.shape[0]
    out = jax.ShapeDtypeStruct((n, D_MODEL), BF16)
    row = pl.BlockSpec((tm, D_MODEL), lambda i: (i, 0))
    return pl.pallas_call(
        _qkv_kernel,
        grid=(n // tm,),
        in_specs=[row, _const_spec((1, D_MODEL)), _const_spec((1, D_MODEL)),
                  _const_spec((D_MODEL, 3 * D_MODEL))],
        out_specs=[row, row, row],
        out_shape=[out, out, out],
        compiler_params=_cparams(("parallel",)),
    )(x2d, g, b, w_qkv)


def _attn_kernel(lam_ref, q_ref, k_ref, v_ref, km_ref, vm_ref, sw_ref, o_ref,
                 sa_ref, sb_ref, m_ref, l_ref, acc_ref):
    tq, tk = ATT_TQ, ATT_TK
    i = pl.program_id(2)
    lane = lax.broadcasted_iota(jnp.int32, (1, LANES), 1)
    q = q_ref[...]
    zero = jnp.zeros_like(q)
    q2 = jnp.concatenate([jnp.where(lane < DA_HEAD_DIM, q, zero),
                          jnp.where(lane >= DA_HEAD_DIM, q, zero)], axis=0)

    def scores(j):
        off = pl.multiple_of(j * tk, tk)
        return _dot_nt(q2, k_ref[pl.ds(off, tk), :])

    def absorb(s, j):
        m_old = m_ref[...]
        m_new = jnp.maximum(m_old, jnp.max(s, axis=1, keepdims=True))
        alpha = jnp.exp(m_old - m_new)
        p = jnp.exp(s - jnp.concatenate([m_new] * (tk // LANES), axis=1))
        psum = p[:, :LANES]
        for c in range(1, tk // LANES):
            psum = psum + p[:, c * LANES:(c + 1) * LANES]
        l_ref[...] = alpha * l_ref[...] + psum
        off = pl.multiple_of(j * tk, tk)
        acc_ref[...] = alpha * acc_ref[...] + _dot(p.astype(BF16), v_ref[pl.ds(off, tk), :])
        m_ref[...] = m_new

    meta_ok = lax.broadcasted_iota(jnp.int32, (1, LANES), 1) < N_META
    sm = jnp.where(meta_ok, _dot_nt(q2, km_ref[...]), -jnp.inf)
    m0 = jnp.broadcast_to(jnp.max(sm, axis=1, keepdims=True), sm.shape)
    p0 = jnp.exp(sm - m0)
    m_ref[...] = m0
    l_ref[...] = p0
    acc_ref[...] = _dot(p0.astype(BF16), vm_ref[...])

    sa_ref[...] = scores(0)

    def body(j, carry):
        @pl.when(j % 2 == 0)
        def _():
            nxt = scores(j + 1)
            absorb(sa_ref[...], j)
            sb_ref[...] = nxt

        @pl.when(j % 2 == 1)
        def _():
            nxt = scores(j + 1)
            absorb(sb_ref[...], j)
            sa_ref[...] = nxt

        return carry

    lax.fori_loop(0, i, body, 0)

    qc = (lax.broadcasted_iota(jnp.int32, (2 * tq, tk), 0) % tq) // 64
    kc = lax.broadcasted_iota(jnp.int32, (2 * tq, tk), 1) // 64
    vis = kc <= qc

    @pl.when(i % 2 == 0)
    def _():
        absorb(jnp.where(vis, sa_ref[...], -jnp.inf), i)

    @pl.when(i % 2 == 1)
    def _():
        absorb(jnp.where(vis, sb_ref[...], -jnp.inf), i)

    a = acc_ref[...] / jnp.sum(l_ref[...], axis=1, keepdims=True)
    o = a[:tq] - lam_ref[0, 0] * a[tq:]
    o = o * lax.rsqrt(jnp.mean(o * o, axis=-1, keepdims=True) + RMS_EPS) * sw_ref[...]
    o_ref[...] = (o * (1.0 - LAMBDA_INIT)).astype(BF16)


def _attn_call(lam, q, k, v, km, vm, subln_w):
    bsz, s, _ = q.shape
    nq = s // ATT_TQ
    qspec = pl.BlockSpec((None, ATT_TQ, LANES), lambda b, h, i: (b, i, h))
    kvspec = pl.BlockSpec((None, s, LANES), lambda b, h, i: (b, 0, h))
    mspec = pl.BlockSpec((LANES, LANES), lambda b, h, i: (0, h))
    return pl.pallas_call(
        _attn_kernel,
        grid=(bsz, DA_HEADS, nq),
        in_specs=[pl.BlockSpec(memory_space=pltpu.SMEM), qspec, kvspec, kvspec, mspec, mspec,
                  _const_spec((1, LANES))],
        out_specs=qspec,
        out_shape=jax.ShapeDtypeStruct((bsz, s, D_MODEL), BF16),
        scratch_shapes=[pltpu.VMEM((2 * ATT_TQ, ATT_TK), F32), pltpu.VMEM((2 * ATT_TQ, ATT_TK), F32),
                        pltpu.VMEM((2 * ATT_TQ, LANES), F32), pltpu.VMEM((2 * ATT_TQ, LANES), F32),
                        pltpu.VMEM((2 * ATT_TQ, LANES), F32)],
        compiler_params=_cparams(("parallel", "parallel", "arbitrary")),
    )(lam, q, k, v, km, vm, subln_w)


def _split3(a):
    hi = a.astype(BF16)
    r = a - hi.astype(F32)
    mid = r.astype(BF16)
    lo = (r - mid.astype(F32)).astype(BF16)
    return hi, mid, lo


def _ssd_kernel(x_ref, meta_ref, g_ref, b_ref, wz_ref, wxbc_ref, wdt_ref, cw_ref, cb_ref,
                dtb_ref, alog_ref, dskip_ref, nw_ref, o_ref, state_ref, cbuf_ref, y_ref):
    lt, lc = SSD_TILE, SSD_CHUNK
    t = pl.program_id(1)
    is_meta = t == 0

    @pl.when(is_meta)
    def _():
        state_ref[...] = jnp.zeros_like(state_ref)
        cbuf_ref[0:8, :] = jnp.zeros((8, SSD_CONV_DIM), F32)

    row = lax.broadcasted_iota(jnp.int32, (lt, 1), 0)
    valid = jnp.logical_or(jnp.logical_not(is_meta), row >= lt - N_META)
    x = jnp.where(is_meta, meta_ref[...], x_ref[...])
    h = _layer_norm(x, g_ref[...], b_ref[...]).astype(BF16)
    z = _dot(h, wz_ref[...])
    xbc = jnp.where(valid, _dot(h, wxbc_ref[...]), 0.0)
    dtr = _dot(h, wdt_ref[...])

    cbuf_ref[8:8 + lt, :] = xbc
    conv = (cw_ref[0:1, :] * cbuf_ref[5:5 + lt, :] + cw_ref[1:2, :] * cbuf_ref[6:6 + lt, :]
            + cw_ref[2:3, :] * cbuf_ref[7:7 + lt, :] + cw_ref[3:4, :] * xbc + cb_ref[...])
    cbuf_ref[0:8, :] = cbuf_ref[lt:lt + 8, :]
    act = jnp.where(valid, conv * _sigmoid(conv), 0.0)
    xs = act[:, :SSD_INNER]
    bm = act[:, SSD_INNER:SSD_INNER + SSD_GROUPS * SSD_STATE]
    cm = act[:, SSD_INNER + SSD_GROUPS * SSD_STATE:].astype(BF16)
    bm_t = bm.T.astype(BF16)

    dtv = dtr + dtb_ref[...]
    dt = jnp.maximum(dtv, 0.0) + jnp.log1p(jnp.exp(-jnp.abs(dtv)))
    dt = jnp.where(valid, dt, 0.0)
    a = dt * (-jnp.exp(alog_ref[...]))

    lane = lax.broadcasted_iota(jnp.int32, (1, LANES), 1)
    left = lane < SSD_HEAD_DIM
    tri_r = lax.broadcasted_iota(jnp.int32, (lc, lc), 0)
    tri_c = lax.broadcasted_iota(jnp.int32, (lc, lc), 1)
    causal = tri_c <= tri_r
    tri = jnp.where(causal, 1.0, 0.0).astype(BF16)

    for c in range(lt // lc):
        rs = slice(c * lc, (c + 1) * lc)
        hi, mid, lo = _split3(a[rs])
        acs = _dot(tri, hi) + _dot(tri, mid) + _dot(tri, lo)
        acs_t = acs.T
        dt_c = dt[rs]
        for g in range(SSD_GROUPS):
            c_g = cm[rs, g * SSD_STATE:(g + 1) * SSD_STATE]
            bt_g = bm_t[g * SSD_STATE:(g + 1) * SSD_STATE, rs]
            cb = _dot(c_g, bt_g)
            for jj in range(2):
                j = 2 * g + jj
                h0, h1 = 2 * j, 2 * j + 1
                col0, col1 = acs[:, h0:h0 + 1], acs[:, h1:h1 + 1]
                row0, row1 = acs_t[h0:h0 + 1, :], acs_t[h1:h1 + 1, :]
                l0 = jnp.exp(jnp.where(causal, col0 - row0, -jnp.inf))
                l1 = jnp.exp(jnp.where(causal, col1 - row1, -jnp.inf))
                mm = jnp.concatenate([(cb * l0).astype(BF16), (cb * l1).astype(BF16)], axis=0)
                xp = xs[rs, j * LANES:(j + 1) * LANES]
                xdt = xp * jnp.where(left, dt_c[:, h0:h0 + 1], dt_c[:, h1:h1 + 1])
                yy = _dot(mm, xdt.astype(BF16))
                y_diag = jnp.where(left, yy[:lc], yy[lc:])
                acs_p = jnp.where(left, col0, col1)
                st = state_ref[j]
                y_off = _dot(c_g, st.astype(BF16)) * jnp.exp(acs_p)
                last_p = jnp.where(left, acs[lc - 1:lc, h0:h0 + 1], acs[lc - 1:lc, h1:h1 + 1])
                xd = (xdt * jnp.exp(last_p - acs_p)).astype(BF16)
                state_ref[j] = st * jnp.exp(last_p) + _dot(bt_g, xd)
                y_ref[rs, j * LANES:(j + 1) * LANES] = (
                    y_diag + y_off + xp * dskip_ref[:, j * LANES:(j + 1) * LANES])

    gy = y_ref[...] * (z * _sigmoid(z))
    gw = SSD_INNER // SSD_GROUPS
    outs = []
    for g in range(SSD_GROUPS):
        gg = gy[:, g * gw:(g + 1) * gw]
        outs.append(gg * lax.rsqrt(jnp.mean(gg * gg, axis=-1, keepdims=True) + RMS_EPS))
    o_ref[...] = (jnp.concatenate(outs, axis=1) * nw_ref[...]).astype(BF16)


def _ssd_call(x3, meta_tile, g, b, wz, wxbc, wdt, cw, cb, dtb, alog, dskip, nw):
    bsz, s, _ = x3.shape
    lt = SSD_TILE
    nt = s // lt
    xspec = pl.BlockSpec((None, lt, D_MODEL), lambda bb, t: (bb, jnp.maximum(t - 1, 0), 0))
    return pl.pallas_call(
        _ssd_kernel,
        grid=(bsz, nt + 1),
        in_specs=[xspec, _const_spec((lt, D_MODEL)), _const_spec((1, D_MODEL)),
                  _const_spec((1, D_MODEL)), _const_spec((D_MODEL, SSD_INNER)),
                  _const_spec((D_MODEL, SSD_CONV_DIM)), _const_spec((D_MODEL, LANES)),
                  _const_spec((SSD_CONV, SSD_CONV_DIM)), _const_spec((1, SSD_CONV_DIM)),
                  _const_spec((1, LANES)), _const_spec((1, LANES)), _const_spec((1, SSD_INNER)),
                  _const_spec((1, SSD_INNER))],
        out_specs=xspec,
        out_shape=jax.ShapeDtypeStruct((bsz, s, SSD_INNER), BF16),
        scratch_shapes=[pltpu.VMEM((SSD_HEADS // 2, SSD_STATE, LANES), F32),
                        pltpu.VMEM((lt + 8, SSD_CONV_DIM), F32),
                        pltpu.VMEM((lt, SSD_INNER), F32)],
        compiler_params=_cparams(("parallel", "arbitrary")),
    )(x3, meta_tile, g, b, wz, wxbc, wdt, cw, cb, dtb, alog, dskip, nw)


def _merge_kernel(x_ref, ssd_ref, da_ref, g0_ref, b0_ref, wg_ref, bg_ref, wso_ref, wdo_ref, wo_ref,
                  g1_ref, b1_ref, wr_ref, br_ref,
                  h1_ref, hp_ref, ri_ref, rg_ref, cnt_ref, carry_ref):
    tm = x_ref.shape[0]
    d = D_MODEL

    @pl.when(pl.program_id(0) == 0)
    def _():
        carry_ref[...] = jnp.zeros_like(carry_ref)

    h = _layer_norm(x_ref[...], g0_ref[...], b0_ref[...])
    gates = _sigmoid(_dot(h.astype(BF16), wg_ref[...]) + bg_ref[...])
    y_ssd = _dot(ssd_ref[...], wso_ref[...])
    y_da = _dot(da_ref[...], wdo_ref[...])
    merged = gates[:, :d] * y_ssd + gates[:, d:] * y_da
    mix = _dot(merged.astype(BF16), wo_ref[...])
    h1 = _layer_norm(DEEPNORM_ALPHA * h + mix, g1_ref[...], b1_ref[...])
    h1_ref[...] = h1

    hb = h1.astype(BF16)
    bits = pltpu.bitcast(hb.astype(F32), jnp.uint32)
    hp_ref[...] = bits[:, :d // 2] | (bits[:, d // 2:] >> 16)

    lane = lax.broadcasted_iota(jnp.int32, (tm, LANES), 1)
    logits = _dot(hb, wr_ref[...]) + br_ref[...]
    logits = jnp.where(lane < N_EXPERTS, logits, -jnp.inf)
    vals, sels = [], []
    ri = jnp.zeros((tm, LANES), jnp.int32)
    for k in range(TOP_K):
        mx = jnp.max(logits, axis=1, keepdims=True)
        idx = jnp.min(jnp.where(logits == mx, lane, LANES), axis=1, keepdims=True)
        sel = lane == idx
        logits = jnp.where(sel, -jnp.inf, logits)
        vals.append(mx)
        sels.append(sel)
        ri = jnp.where(lane == k, idx, ri)
    exps = [jnp.exp(v - vals[0]) for v in vals]
    den = exps[0] + exps[1] + exps[2] + exps[3]
    rg = jnp.zeros((tm, LANES), F32)
    for k in range(TOP_K):
        rg = jnp.where(lane == k, exps[k] / den, rg)
    rg_ref[...] = rg

    onehot = jnp.where(sels[0] | sels[1] | sels[2] | sels[3], 1.0, 0.0)
    r_i = lax.broadcasted_iota(jnp.int32, (tm, tm), 0)
    c_i = lax.broadcasted_iota(jnp.int32, (tm, tm), 1)
    strict = jnp.where(c_i < r_i, 1.0, 0.0).astype(BF16)
    before = _dot(strict, onehot.astype(BF16)) + carry_ref[0:1, :]
    for k in range(TOP_K):
        rk = jnp.sum(jnp.where(sels[k], before, 0.0), axis=1, keepdims=True)
        ri = jnp.where(lane == TOP_K + k, rk.astype(jnp.int32), ri)
    ri_ref[...] = ri
    total = carry_ref[...] + jnp.sum(onehot, axis=0, keepdims=True)
    carry_ref[...] = total
    cnt_ref[...] = total


def _merge_call(x2d, ssd_n, da_n, g0, b0, wg, bg, wso, wdo, wo, g1, b1, wr, br, tm):
    n = x2d.shape[0]
    d = D_MODEL
    rowf = pl.BlockSpec((tm, d), lambda i: (i, 0))
    rowl = pl.BlockSpec((tm, LANES), lambda i: (i, 0))
    return pl.pallas_call(
        _merge_kernel,
        grid=(n // tm,),
        in_specs=[rowf, rowf, rowf, _const_spec((1, d)), _const_spec((1, d)),
                  _const_spec((d, 2 * d)), _const_spec((1, 2 * d)), _const_spec((d, d)),
                  _const_spec((d, d)), _const_spec((d, d)), _const_spec((1, d)), _const_spec((1, d)),
                  _const_spec((d, LANES)), _const_spec((1, LANES))],
        out_specs=[rowf, pl.BlockSpec((tm, d // 2), lambda i: (i, 0)), rowl, rowl,
                   _const_spec((8, LANES))],
        out_shape=[jax.ShapeDtypeStruct((n, d), F32), jax.ShapeDtypeStruct((n, d // 2), jnp.uint32),
                   jax.ShapeDtypeStruct((n, LANES), jnp.int32), jax.ShapeDtypeStruct((n, LANES), F32),
                   jax.ShapeDtypeStruct((8, LANES), F32)],
        scratch_shapes=[pltpu.VMEM((8, LANES), F32)],
        compiler_params=_cparams(("arbitrary",)),
    )(x2d, ssd_n, da_n, g0, b0, wg, bg, wso, wdo, wo, g1, b1, wr, br)


def _row_copy(src, dst, sem):
    return pltpu.make_async_copy(src, dst, sem)


def _dispatch_kernel(dest_ref, hp_ref, xs_ref, sem):
    tm = hp_ref.shape[0]

    def issue(r, carry):
        for k in range(TOP_K):
            d = dest_ref[0, r * TOP_K + k]
            _row_copy(hp_ref.at[pl.ds(r, 1), :], xs_ref.at[pl.ds(d, 1), :], sem).start()
        return carry

    lax.fori_loop(0, tm, issue, 0)

    def drain(r, carry):
        for k in range(TOP_K):
            d = dest_ref[0, r * TOP_K + k]
            _row_copy(hp_ref.at[pl.ds(r, 1), :], xs_ref.at[pl.ds(d, 1), :], sem).wait()
        return carry

    lax.fori_loop(0, tm, drain, 0)


def _dispatch_call(dest3, hp, tm):
    n, w = hp.shape
    return pl.pallas_call(
        _dispatch_kernel,
        grid=(n // tm,),
        in_specs=[pl.BlockSpec((None, 1, tm * TOP_K), lambda i: (i, 0, 0), memory_space=pltpu.SMEM),
                  pl.BlockSpec((tm, w), lambda i: (i, 0))],
        out_specs=pl.BlockSpec(memory_space=pl.ANY),
        out_shape=jax.ShapeDtypeStruct((n * TOP_K, w), jnp.uint32),
        scratch_shapes=[pltpu.SemaphoreType.DMA(())],
        compiler_params=_cparams(("arbitrary",)),
    )(dest3, hp)


def _expert_kernel(tile_ref, exp_ref, lo_ref, hi_ref, xs_ref, wgu_ref, bgu_ref, wd_ref, bd_ref, ys_ref):
    i = pl.program_id(0)
    lo = lo_ref[i]
    hi = hi_ref[i]
    tm = xs_ref.shape[0]

    @pl.when(lo < hi)
    def _():
        w = xs_ref[...]
        x_hi = pltpu.bitcast(w & jnp.uint32(0xFFFF0000), F32).astype(BF16)
        x_lo = pltpu.bitcast(w << 16, F32).astype(BF16)
        x = jnp.concatenate([x_hi, x_lo], axis=1)
        hid = _dot(x, wgu_ref[...]) + bgu_ref[...]
        gate = jnp.minimum(hid[:, :D_FF], SWIGLU_LIMIT)
        up = jnp.clip(hid[:, D_FF:], -SWIGLU_LIMIT, SWIGLU_LIMIT)
        act = gate * _sigmoid(SWIGLU_ALPHA * gate) * (up + 1.0)
        y = _dot(act.astype(BF16), wd_ref[...]) + bd_ref[...]
        row = lax.broadcasted_iota(jnp.int32, (tm, 1), 0)
        mine = jnp.logical_and(row >= lo, row < hi)

        @pl.when(lo == 0)
        def _():
            ys_ref[...] = jnp.where(mine, y, 0.0)

        @pl.when(lo > 0)
        def _():
            ys_ref[...] = jnp.where(mine, y, ys_ref[...])


def _expert_call(item_tile, item_exp, item_lo, item_hi, xs, wgu, bgu, wd, bd):
    m, w = xs.shape
    tm = EXPERT_TILE
    n_items = item_tile.shape[0]
    grid_spec = pltpu.PrefetchScalarGridSpec(
        num_scalar_prefetch=4,
        grid=(n_items,),
        in_specs=[pl.BlockSpec((tm, w), lambda i, t, e, lo, hi: (t[i], 0)),
                  pl.BlockSpec((None, D_MODEL, 2 * D_FF), lambda i, t, e, lo, hi: (e[i], 0, 0)),
                  pl.BlockSpec((None, 1, 2 * D_FF), lambda i, t, e, lo, hi: (e[i], 0, 0)),
                  pl.BlockSpec((None, D_FF, D_MODEL), lambda i, t, e, lo, hi: (e[i], 0, 0)),
                  pl.BlockSpec((None, 1, D_MODEL), lambda i, t, e, lo, hi: (e[i], 0, 0))],
        out_specs=pl.BlockSpec((tm, D_MODEL), lambda i, t, e, lo, hi: (t[i], 0)),
    )
    return pl.pallas_call(
        _expert_kernel,
        grid_spec=grid_spec,
        out_shape=jax.ShapeDtypeStruct((m, D_MODEL), F32),
        compiler_params=_cparams(("arbitrary",)),
    )(item_tile, item_exp, item_lo, item_hi, xs, wgu, bgu, wd, bd)


def _combine_kernel(dest_ref, ys_ref, h1_ref, rg_ref, g2_ref, b2_ref, o_ref, buf_ref, sem):
    tm = h1_ref.shape[0]

    def issue(r, carry):
        for k in range(TOP_K):
            d = dest_ref[0, r * TOP_K + k]
            _row_copy(ys_ref.at[pl.ds(d, 1), :], buf_ref.at[k, pl.ds(r, 1), :], sem).start()
        return carry

    lax.fori_loop(0, tm, issue, 0)

    def drain(r, carry):
        for k in range(TOP_K):
            d = dest_ref[0, r * TOP_K + k]
            _row_copy(ys_ref.at[pl.ds(d, 1), :], buf_ref.at[k, pl.ds(r, 1), :], sem).wait()
        return carry

    lax.fori_loop(0, tm, drain, 0)

    rg = rg_ref[...]
    ffn = rg[:, 0:1] * buf_ref[0]
    for k in range(1, TOP_K):
        ffn = ffn + rg[:, k:k + 1] * buf_ref[k]
    o_ref[...] = _layer_norm(DEEPNORM_ALPHA * h1_ref[...] + ffn, g2_ref[...], b2_ref[...])


def _combine_call(dest3, ys, h1, rg, g2, b2, tm):
    n, d = h1.shape
    rowf = pl.BlockSpec((tm, d), lambda i: (i, 0))
    return pl.pallas_call(
        _combine_kernel,
        grid=(n // tm,),
        in_specs=[pl.BlockSpec((None, 1, tm * TOP_K), lambda i: (i, 0, 0), memory_space=pltpu.SMEM),
                  pl.BlockSpec(memory_space=pl.ANY), rowf,
                  pl.BlockSpec((tm, LANES), lambda i: (i, 0)), _const_spec((1, d)), _const_spec((1, d))],
        out_specs=rowf,
        out_shape=jax.ShapeDtypeStruct((n, d), F32),
        scratch_shapes=[pltpu.VMEM((TOP_K, tm, d), F32), pltpu.SemaphoreType.DMA(())],
        compiler_params=_cparams(("arbitrary",)),
    )(dest3, ys, h1, rg, g2, b2)


def _work_items(counts, m):
    tm = EXPERT_TILE
    n_tiles = m // tm
    max_items = n_tiles + N_EXPERTS - 1
    grp_end = jnp.cumsum(counts)
    grp_start = grp_end - counts
    first_tile = grp_start // tm
    last_tile = (grp_end - 1) // tm
    n_e = jnp.where(counts > 0, last_tile - first_tile + 1, 0)
    item_end = jnp.cumsum(n_e)
    item_start = item_end - n_e
    total = item_end[-1]
    i = jnp.arange(max_items, dtype=jnp.int32)
    valid = i < total
    ic = jnp.minimum(i, total - 1)
    e = jnp.minimum(jnp.searchsorted(item_end, ic, side="right"), N_EXPERTS - 1).astype(jnp.int32)
    tile = first_tile[e] + (ic - item_start[e])
    lo = jnp.maximum(grp_start[e], tile * tm) - tile * tm
    hi = jnp.minimum(grp_end[e], (tile + 1) * tm) - tile * tm
    zero = jnp.zeros_like(lo)
    return (tile.astype(jnp.int32), e, jnp.where(valid, lo, zero).astype(jnp.int32),
            jnp.where(valid, hi, zero).astype(jnp.int32), grp_start)


def kernel(x, meta_tokens, ln_in_g, ln_in_b, w_in, b_gate, conv_w, conv_b, dt_bias, a_log, d_skip, ssd_norm_w, w_ssd_out, lam_q1, lam_k1, lam_q2, lam_k2, subln_w, w_da_out, w_out, ln1_g, ln1_b, w_router, b_router, w_gate_up, b_gate_up, w_down, b_down, ln2_g, ln2_b):
    bsz, s, d = x.shape
    n = bsz * s
    l = 0
    row = lambda v: v.reshape(1, -1).astype(F32)

    w = w_in[l]
    c0 = SSD_INNER
    c1 = c0 + SSD_CONV_DIM
    c2 = c1 + SSD_HEADS
    c3 = c2 + 3 * D_MODEL
    w_z = w[:, :c0].astype(BF16)
    w_xbc = w[:, c0:c1].astype(BF16)
    w_dt = jnp.pad(w[:, c1:c2], ((0, 0), (0, LANES - SSD_HEADS))).astype(BF16)
    w_qkv = w[:, c2:c3].astype(BF16)
    w_g = w[:, c3:].astype(BF16)
    g0, b0 = row(ln_in_g), row(ln_in_b)

    x2d = x.reshape(n, d)
    q, k, v = _qkv_call(x2d, g0, b0, w_qkv, ROW_TILE)
    _, km, vm = _qkv_call(meta_tokens.astype(F32), g0, b0, w_qkv, N_META)
    km = jnp.pad(km, ((0, LANES - N_META), (0, 0)))
    vm = jnp.pad(vm, ((0, LANES - N_META), (0, 0)))

    lam = (jnp.exp(jnp.sum(lam_q1[l].astype(F32) * lam_k1[l].astype(F32)))
           - jnp.exp(jnp.sum(lam_q2[l].astype(F32) * lam_k2[l].astype(F32))) + LAMBDA_INIT)
    da_n = _attn_call(lam.reshape(1, 1), q.reshape(bsz, s, d), k.reshape(bsz, s, d),
                      v.reshape(bsz, s, d), km, vm, row(subln_w[l]))

    meta_tile = jnp.pad(meta_tokens.astype(F32), ((SSD_TILE - N_META, 0), (0, 0)))
    pad_h = lambda vec: jnp.pad(row(vec), ((0, 0), (0, LANES - SSD_HEADS)))
    ssd_n = _ssd_call(x, meta_tile, g0, b0, w_z, w_xbc, w_dt, conv_w[l].astype(F32), row(conv_b[l]),
                      pad_h(dt_bias[l]), pad_h(a_log[l]),
                      row(jnp.repeat(d_skip[l].astype(F32), SSD_HEAD_DIM)), row(ssd_norm_w[l]))

    w_r = jnp.pad(w_router[l], ((0, 0), (0, LANES - N_EXPERTS))).astype(BF16)
    b_r = jnp.pad(row(b_router[l]), ((0, 0), (0, LANES - N_EXPERTS)))
    h1, hp, ri, rg, cnt = _merge_call(
        x2d, ssd_n.reshape(n, d), da_n.reshape(n, d), g0, b0, w_g, row(b_gate[l]),
        w_ssd_out[l].astype(BF16), w_da_out[l].astype(BF16), w_out[l].astype(BF16),
        row(ln1_g[l]), row(ln1_b[l]), w_r, b_r, ROW_TILE)

    counts = cnt[0, :N_EXPERTS].astype(jnp.int32)
    m = n * TOP_K
    item_tile, item_exp, item_lo, item_hi, grp_start = _work_items(counts, m)
    dest = grp_start[ri[:, :TOP_K]] + ri[:, TOP_K:2 * TOP_K]
    dest3 = dest.astype(jnp.int32).reshape(n // MOVE_TILE, 1, MOVE_TILE * TOP_K)

    xs = _dispatch_call(dest3, hp, MOVE_TILE)
    ys = _expert_call(item_tile, item_exp, item_lo, item_hi, xs,
                      w_gate_up[l].astype(BF16), b_gate_up[l].reshape(N_EXPERTS, 1, -1).astype(F32),
                      w_down[l].astype(BF16), b_down[l].reshape(N_EXPERTS, 1, -1).astype(F32))
    out = _combine_call(dest3, ys, h1, rg, row(ln2_g[l]), row(ln2_b[l]), MOVE_TILE)
    return out.reshape(bsz, s, d)
```

```python
import functools
import math

import jax
import jax.numpy as jnp
from jax import lax
from jax.experimental import pallas as pl
from jax.experimental.pallas import tpu as pltpu

F32 = jnp.float32
BF16 = jnp.bfloat16

D_MODEL = 1024
N_META = 16
SSD_HEADS = 16
SSD_HEAD_DIM = 64
SSD_INNER = 1024
SSD_GROUPS = 4
SSD_STATE = 128
SSD_CONV = 4
SSD_CONV_DIM = 2048
DA_HEADS = 8
DA_HEAD_DIM = 64
N_EXPERTS = 32
TOP_K = 4
D_FF = 1024
SWIGLU_LIMIT = 7.0
SWIGLU_ALPHA = 1.702
DEPTH = 1
DEEPNORM_ALPHA = (2.0 * DEPTH) ** 0.25
LN_EPS = 1e-5
RMS_EPS = 1e-6
LAMBDA_INIT = 0.8 - 0.6 * math.exp(-0.3 * 0)
LOG2_E = math.log2(math.e)

LANES = 128
VMEM_LIMIT = 56 * 1024 * 1024

ROW_TILE = 512
ATT_TQ = 512
ATT_TK = 512
SSD_TILE = 256
SSD_CHUNK = 128
EXPERT_TILE = 512
MOVE_TILE = 256


def _cparams(sem):
    return pltpu.CompilerParams(dimension_semantics=sem, vmem_limit_bytes=VMEM_LIMIT)


def _const_spec(shape):
    nd = len(shape)
    return pl.BlockSpec(shape, lambda *a: (0,) * nd)


def _layer_norm(x, g, b):
    mu = jnp.mean(x, axis=-1, keepdims=True)
    xc = x - mu
    var = jnp.mean(xc * xc, axis=-1, keepdims=True)
    return xc * lax.rsqrt(var + LN_EPS) * g + b


def _sigmoid(x):
    return 1.0 / (1.0 + jnp.exp(-x))


def _dot(a, b):
    return jnp.dot(a, b, preferred_element_type=F32)


def _dot_nt(a, b):
    return lax.dot_general(a, b, (((1,), (1,)), ((), ())), preferred_element_type=F32)


def _qkv_kernel(x_ref, g_ref, b_ref, w_ref, q_ref, k_ref, v_ref):
    h = _layer_norm(x_ref[...], g_ref[...], b_ref[...]).astype(BF16)
    acc = _dot(h, w_ref[...])
    d = D_MODEL
    q_ref[...] = (acc[:, :d] * (DA_HEAD_DIM ** -0.5 * LOG2_E)).astype(BF16)
    k_ref[...] = acc[:, d:2 * d].astype(BF16)
    v_ref[...] = acc[:, 2 * d:].astype(BF16)


def _qkv_call(x2d, g, b, w_qkv, tm):
    n = x2d.shape[0]
    out = jax.ShapeDtypeStruct((n, D_MODEL), BF16)
    row = pl.BlockSpec((tm, D_MODEL), lambda i: (i, 0))
    return pl.pallas_call(
        _qkv_kernel,
        grid=(n // tm,),
        in_specs=[row, _const_spec((1, D_MODEL)), _const_spec((1, D_MODEL)),
                  _const_spec((D_MODEL, 3 * D_MODEL))],
        out_specs=[row, row, row],
        out_shape=[out, out, out],
        compiler_params=_cparams(("parallel",)),
    )(x2d, g, b, w_qkv)


def _attn_kernel(lam_ref, q_ref, k_ref, v_ref, km_ref, vm_ref, sw_ref, o_ref,
                 sa_ref, sb_ref, m_ref, l_ref, acc_ref):
    tq, tk = ATT_TQ, ATT_TK
    i = pl.program_id(2)
    lane = lax.broadcasted_iota(jnp.int32, (1, LANES), 1)
    q = q_ref[...]
    zero = jnp.zeros_like(q)
    q2 = jnp.concatenate([jnp.where(lane < DA_HEAD_DIM, q, zero),
                          jnp.where(lane >= DA_HEAD_DIM, q, zero)], axis=0)

    def scores(j):
        off = pl.multiple_of(j * tk, tk)
        return _dot_nt(q2, k_ref[pl.ds(off, tk), :])

    def absorb(s, j):
        m_old = m_ref[...]
        m_new = jnp.maximum(m_old, jnp.max(s, axis=1, keepdims=True))
        alpha = jnp.exp2(m_old - m_new)
        p = jnp.exp2(s - jnp.concatenate([m_new] * (tk // LANES), axis=1))
        psum = p[:, :LANES]
        for c in range(1, tk // LANES):
            psum = psum + p[:, c * LANES:(c + 1) * LANES]
        l_ref[...] = alpha * l_ref[...] + psum
        off = pl.multiple_of(j * tk, tk)
        acc_ref[...] = alpha * acc_ref[...] + _dot(p.astype(BF16), v_ref[pl.ds(off, tk), :])
        m_ref[...] = m_new

    meta_ok = lax.broadcasted_iota(jnp.int32, (1, LANES), 1) < N_META
    sm = jnp.where(meta_ok, _dot_nt(q2, km_ref[...]), -jnp.inf)
    m0 = jnp.broadcast_to(jnp.max(sm, axis=1, keepdims=True), sm.shape)
    p0 = jnp.exp2(sm - m0)
    m_ref[...] = m0
    l_ref[...] = p0
    acc_ref[...] = _dot(p0.astype(BF16), vm_ref[...])

    sa_ref[...] = scores(0)

    def body(j, carry):
        @pl.when(j % 2 == 0)
        def _():
            nxt = scores(j + 1)
            absorb(sa_ref[...], j)
            sb_ref[...] = nxt

        @pl.when(j % 2 == 1)
        def _():
            nxt = scores(j + 1)
            absorb(sb_ref[...], j)
            sa_ref[...] = nxt

        return carry

    lax.fori_loop(0, i, body, 0)

    qc = (lax.broadcasted_iota(jnp.int32, (2 * tq, tk), 0) % tq) // 64
    kc = lax.broadcasted_iota(jnp.int32, (2 * tq, tk), 1) // 64
    vis = kc <= qc

    @pl.when(i % 2 == 0)
    def _():
        absorb(jnp.where(vis, sa_ref[...], -jnp.inf), i)

    @pl.when(i % 2 == 1)
    def _():
        absorb(jnp.where(vis, sb_ref[...], -jnp.inf), i)

    a = acc_ref[...] / jnp.sum(l_ref[...], axis=1, keepdims=True)
    o = a[:tq] - lam_ref[0, 0] * a[tq:]
    o = o * lax.rsqrt(jnp.mean(o * o, axis=-1, keepdims=True) + RMS_EPS) * sw_ref[...]
    o_ref[...] = (o * (1.0 - LAMBDA_INIT)).astype(BF16)


def _attn_call(lam, q, k, v, km, vm, subln_w):
    bsz, s, _ = q.shape
    nq = s // ATT_TQ
    qspec = pl.BlockSpec((None, ATT_TQ, LANES), lambda b, h, i: (b, i, h))
    kvspec = pl.BlockSpec((None, s, LANES), lambda b, h, i: (b, 0, h))
    mspec = pl.BlockSpec((LANES, LANES), lambda b, h, i: (0, h))
    return pl.pallas_call(
        _attn_kernel,
        grid=(bsz, DA_HEADS, nq),
        in_specs=[pl.BlockSpec(memory_space=pltpu.SMEM), qspec, kvspec, kvspec, mspec, mspec,
                  _const_spec((1, LANES))],
        out_specs=qspec,
        out_shape=jax.ShapeDtypeStruct((bsz, s, D_MODEL), BF16),
        scratch_shapes=[pltpu.VMEM((2 * ATT_TQ, ATT_TK), F32), pltpu.VMEM((2 * ATT_TQ, ATT_TK), F32),
                        pltpu.VMEM((2 * ATT_TQ, LANES), F32), pltpu.VMEM((2 * ATT_TQ, LANES), F32),
                        pltpu.VMEM((2 * ATT_TQ, LANES), F32)],
        compiler_params=_cparams(("parallel", "parallel", "arbitrary")),
    )(lam, q, k, v, km, vm, subln_w)


def _split3(a):
    hi = a.astype(BF16)
    r = a - hi.astype(F32)
    mid = r.astype(BF16)
    lo = (r - mid.astype(F32)).astype(BF16)
    return hi, mid, lo


def _ssd_kernel(x_ref, meta_ref, g_ref, b_ref, wz_ref, wxbc_ref, wdt_ref, cw_ref, cb_ref,
                dtb_ref, alog_ref, dskip_ref, nw_ref, o_ref, state_ref, cbuf_ref, y_ref):
    lt, lc = SSD_TILE, SSD_CHUNK
    t = pl.program_id(1)
    is_meta = t == 0

    @pl.when(is_meta)
    def _():
        state_ref[...] = jnp.zeros_like(state_ref)
        cbuf_ref[0:8, :] = jnp.zeros((8, SSD_CONV_DIM), F32)

    row = lax.broadcasted_iota(jnp.int32, (lt, 1), 0)
    valid = jnp.logical_or(jnp.logical_not(is_meta), row >= lt - N_META)
    x = jnp.where(is_meta, meta_ref[...], x_ref[...])
    h = _layer_norm(x, g_ref[...], b_ref[...]).astype(BF16)
    z = _dot(h, wz_ref[...])
    xbc = jnp.where(valid, _dot(h, wxbc_ref[...]), 0.0)
    dtr = _dot(h, wdt_ref[...])

    cbuf_ref[8:8 + lt, :] = xbc
    conv = (cw_ref[0:1, :] * cbuf_ref[5:5 + lt, :] + cw_ref[1:2, :] * cbuf_ref[6:6 + lt, :]
            + cw_ref[2:3, :] * cbuf_ref[7:7 + lt, :] + cw_ref[3:4, :] * xbc + cb_ref[...])
    cbuf_ref[0:8, :] = cbuf_ref[lt:lt + 8, :]
    act = jnp.where(valid, conv * _sigmoid(conv), 0.0)
    xs = act[:, :SSD_INNER]
    bm = act[:, SSD_INNER:SSD_INNER + SSD_GROUPS * SSD_STATE]
    cm = act[:, SSD_INNER + SSD_GROUPS * SSD_STATE:].astype(BF16)
    bm_t = bm.T.astype(BF16)

    dtv = dtr + dtb_ref[...]
    dt = jnp.maximum(dtv, 0.0) + jnp.log1p(jnp.exp(-jnp.abs(dtv)))
    dt = jnp.where(valid, dt, 0.0)
    a = dt * (-jnp.exp(alog_ref[...]))

    lane = lax.broadcasted_iota(jnp.int32, (1, LANES), 1)
    left = lane < SSD_HEAD_DIM
    tri_r = lax.broadcasted_iota(jnp.int32, (lc, lc), 0)
    tri_c = lax.broadcasted_iota(jnp.int32, (lc, lc), 1)
    causal = tri_c <= tri_r
    tri = jnp.where(causal, 1.0, 0.0).astype(BF16)

    for c in range(lt // lc):
        rs = slice(c * lc, (c + 1) * lc)
        hi, mid, lo = _split3(a[rs])
        acs = _dot(tri, hi) + _dot(tri, mid) + _dot(tri, lo)
        acs_t = acs.T
        dt_c = dt[rs]
        for g in range(SSD_GROUPS):
            c_g = cm[rs, g * SSD_STATE:(g + 1) * SSD_STATE]
            bt_g = bm_t[g * SSD_STATE:(g + 1) * SSD_STATE, rs]
            cb = _dot(c_g, bt_g)
            for jj in range(2):
                j = 2 * g + jj
                h0, h1 = 2 * j, 2 * j + 1
                col0, col1 = acs[:, h0:h0 + 1], acs[:, h1:h1 + 1]
                row0, row1 = acs_t[h0:h0 + 1, :], acs_t[h1:h1 + 1, :]
                l0 = jnp.exp(jnp.where(causal, col0 - row0, -jnp.inf))
                l1 = jnp.exp(jnp.where(causal, col1 - row1, -jnp.inf))
                mm = jnp.concatenate([(cb * l0).astype(BF16), (cb * l1).astype(BF16)], axis=0)
                xp = xs[rs, j * LANES:(j + 1) * LANES]
                xdt = xp * jnp.where(left, dt_c[:, h0:h0 + 1], dt_c[:, h1:h1 + 1])
                yy = _dot(mm, xdt.astype(BF16))
                y_diag = jnp.where(left, yy[:lc], yy[lc:])
                acs_p = jnp.where(left, col0, col1)
                st = state_ref[j]
                y_off = _dot(c_g, st.astype(BF16)) * jnp.exp(acs_p)
                last_p = jnp.where(left, acs[lc - 1:lc, h0:h0 + 1], acs[lc - 1:lc, h1:h1 + 1])
                xd = (xdt * jnp.exp(last_p - acs_p)).astype(BF16)
                state_ref[j] = st * jnp.exp(last_p) + _dot(bt_g, xd)
                y_ref[rs, j * LANES:(j + 1) * LANES] = (
                    y_diag + y_off + xp * dskip_ref[:, j * LANES:(j + 1) * LANES])

    gy = y_ref[...] * (z * _sigmoid(z))
    gw = SSD_INNER // SSD_GROUPS
    outs = []
    for g in range(SSD_GROUPS):
        gg = gy[:, g * gw:(g + 1) * gw]
        outs.append(gg * lax.rsqrt(jnp.mean(gg * gg, axis=-1, keepdims=True) + RMS_EPS))
    o_ref[...] = (jnp.concatenate(outs, axis=1) * nw_ref[...]).astype(BF16)


def _ssd_call(x3, meta_tile, g, b, wz, wxbc, wdt, cw, cb, dtb, alog, dskip, nw):
    bsz, s, _ = x3.shape
    lt = SSD_TILE
    nt = s // lt
    xspec = pl.BlockSpec((None, lt, D_MODEL), lambda bb, t: (bb, jnp.maximum(t - 1, 0), 0))
    return pl.pallas_call(
        _ssd_kernel,
        grid=(bsz, nt + 1),
        in_specs=[xspec, _const_spec((lt, D_MODEL)), _const_spec((1, D_MODEL)),
                  _const_spec((1, D_MODEL)), _const_spec((D_MODEL, SSD_INNER)),
                  _const_spec((D_MODEL, SSD_CONV_DIM)), _const_spec((D_MODEL, LANES)),
                  _const_spec((SSD_CONV, SSD_CONV_DIM)), _const_spec((1, SSD_CONV_DIM)),
                  _const_spec((1, LANES)), _const_spec((1, LANES)), _const_spec((1, SSD_INNER)),
                  _const_spec((1, SSD_INNER))],
        out_specs=xspec,
        out_shape=jax.ShapeDtypeStruct((bsz, s, SSD_INNER), BF16),
        scratch_shapes=[pltpu.VMEM((SSD_HEADS // 2, SSD_STATE, LANES), F32),
                        pltpu.VMEM((lt + 8, SSD_CONV_DIM), F32),
                        pltpu.VMEM((lt, SSD_INNER), F32)],
        compiler_params=_cparams(("parallel", "arbitrary")),
    )(x3, meta_tile, g, b, wz, wxbc, wdt, cw, cb, dtb, alog, dskip, nw)


def _merge_kernel(x_ref, ssd_ref, da_ref, g0_ref, b0_ref, wg_ref, bg_ref, wso_ref, wdo_ref, wo_ref,
                  g1_ref, b1_ref, wr_ref, br_ref,
                  h1_ref, hp_ref, ri_ref, rg_ref, cnt_ref, carry_ref):
    tm = x_ref.shape[0]
    d = D_MODEL

    @pl.when(pl.program_id(0) == 0)
    def _():
        carry_ref[...] = jnp.zeros_like(carry_ref)

    h = _layer_norm(x_ref[...], g0_ref[...], b0_ref[...])
    gates = _sigmoid(_dot(h.astype(BF16), wg_ref[...]) + bg_ref[...])
    y_ssd = _dot(ssd_ref[...], wso_ref[...])
    y_da = _dot(da_ref[...], wdo_ref[...])
    merged = gates[:, :d] * y_ssd + gates[:, d:] * y_da
    mix = _dot(merged.astype(BF16), wo_ref[...])
    h1 = _layer_norm(DEEPNORM_ALPHA * h + mix, g1_ref[...], b1_ref[...])
    h1_ref[...] = h1

    hb = h1.astype(BF16)
    bits = pltpu.bitcast(hb.astype(F32), jnp.uint32)
    hp_ref[...] = bits[:, :d // 2] | (bits[:, d // 2:] >> 16)

    lane = lax.broadcasted_iota(jnp.int32, (tm, LANES), 1)
    logits = _dot(hb, wr_ref[...]) + br_ref[...]
    logits = jnp.where(lane < N_EXPERTS, logits, -jnp.inf)
    vals, sels = [], []
    ri = jnp.zeros((tm, LANES), jnp.int32)
    for k in range(TOP_K):
        mx = jnp.max(logits, axis=1, keepdims=True)
        idx = jnp.min(jnp.where(logits == mx, lane, LANES), axis=1, keepdims=True)
        sel = lane == idx
        logits = jnp.where(sel, -jnp.inf, logits)
        vals.append(mx)
        sels.append(sel)
        ri = jnp.where(lane == k, idx, ri)
    exps = [jnp.exp(v - vals[0]) for v in vals]
    den = exps[0] + exps[1] + exps[2] + exps[3]
    rg = jnp.zeros((tm, LANES), F32)
    for k in range(TOP_K):
        rg = jnp.where(lane == k, exps[k] / den, rg)
    rg_ref[...] = rg

    onehot = jnp.where(sels[0] | sels[1] | sels[2] | sels[3], 1.0, 0.0)
    r_i = lax.broadcasted_iota(jnp.int32, (tm, tm), 0)
    c_i = lax.broadcasted_iota(jnp.int32, (tm, tm), 1)
    strict = jnp.where(c_i < r_i, 1.0, 0.0).astype(BF16)
    before = _dot(strict, onehot.astype(BF16)) + carry_ref[0:1, :]
    for k in range(TOP_K):
        rk = jnp.sum(jnp.where(sels[k], before, 0.0), axis=1, keepdims=True)
        ri = jnp.where(lane == TOP_K + k, rk.astype(jnp.int32), ri)
    ri_ref[...] = ri
    total = carry_ref[...] + jnp.sum(onehot, axis=0, keepdims=True)
    carry_ref[...] = total
    cnt_ref[...] = total


def _merge_call(x2d, ssd_n, da_n, g0, b0, wg, bg, wso, wdo, wo, g1, b1, wr, br, tm):
    n = x2d.shape[0]
    d = D_MODEL
    rowf = pl.BlockSpec((tm, d), lambda i: (i, 0))
    rowl = pl.BlockSpec((tm, LANES), lambda i: (i, 0))
    return pl.pallas_call(
        _merge_kernel,
        grid=(n // tm,),
        in_specs=[rowf, rowf, rowf, _const_spec((1, d)), _const_spec((1, d)),
                  _const_spec((d, 2 * d)), _const_spec((1, 2 * d)), _const_spec((d, d)),
                  _const_spec((d, d)), _const_spec((d, d)), _const_spec((1, d)), _const_spec((1, d)),
                  _const_spec((d, LANES)), _const_spec((1, LANES))],
        out_specs=[rowf, pl.BlockSpec((tm, d // 2), lambda i: (i, 0)), rowl, rowl,
                   _const_spec((8, LANES))],
        out_shape=[jax.ShapeDtypeStruct((n, d), F32), jax.ShapeDtypeStruct((n, d // 2), jnp.uint32),
                   jax.ShapeDtypeStruct((n, LANES), jnp.int32), jax.ShapeDtypeStruct((n, LANES), F32),
                   jax.ShapeDtypeStruct((8, LANES), F32)],
        scratch_shapes=[pltpu.VMEM((8, LANES), F32)],
        compiler_params=_cparams(("arbitrary",)),
    )(x2d, ssd_n, da_n, g0, b0, wg, bg, wso, wdo, wo, g1, b1, wr, br)


def _row_copy(src, dst, sem):
    return pltpu.make_async_copy(src, dst, sem)


def _dispatch_kernel(dest_ref, hp_ref, xs_ref, sem):
    tm = hp_ref.shape[0]

    def issue(r, carry):
        for k in range(TOP_K):
            d = dest_ref[0, r * TOP_K + k]
            _row_copy(hp_ref.at[pl.ds(r, 1), :], xs_ref.at[pl.ds(d, 1), :], sem).start()
        return carry

    lax.fori_loop(0, tm, issue, 0)

    def drain(r, carry):
        for k in range(TOP_K):
            d = dest_ref[0, r * TOP_K + k]
            _row_copy(hp_ref.at[pl.ds(r, 1), :], xs_ref.at[pl.ds(d, 1), :], sem).wait()
        return carry

    lax.fori_loop(0, tm, drain, 0)


def _dispatch_call(dest3, hp, tm):
    n, w = hp.shape
    return pl.pallas_call(
        _dispatch_kernel,
        grid=(n // tm,),
        in_specs=[pl.BlockSpec((None, 1, tm * TOP_K), lambda i: (i, 0, 0), memory_space=pltpu.SMEM),
                  pl.BlockSpec((tm, w), lambda i: (i, 0))],
        out_specs=pl.BlockSpec(memory_space=pl.ANY),
        out_shape=jax.ShapeDtypeStruct((n * TOP_K, w), jnp.uint32),
        scratch_shapes=[pltpu.SemaphoreType.DMA(())],
        compiler_params=_cparams(("arbitrary",)),
    )(dest3, hp)


def _expert_kernel(tile_ref, exp_ref, lo_ref, hi_ref, xs_ref, wgu_ref, bgu_ref, wd_ref, bd_ref, ys_ref,
                   wgu_bf_ref, wd_bf_ref):
    i = pl.program_id(0)
    lo = lo_ref[i]
    hi = hi_ref[i]
    tm = xs_ref.shape[0]

    @pl.when(jnp.logical_or(i == 0, exp_ref[i] != exp_ref[jnp.maximum(i - 1, 0)]))
    def _():
        wgu_bf_ref[...] = wgu_ref[...].astype(BF16)
        wd_bf_ref[...] = wd_ref[...].astype(BF16)

    @pl.when(lo < hi)
    def _():
        w = xs_ref[...]
        x_hi = pltpu.bitcast(w & jnp.uint32(0xFFFF0000), F32).astype(BF16)
        x_lo = pltpu.bitcast(w << 16, F32).astype(BF16)
        x = jnp.concatenate([x_hi, x_lo], axis=1)
        hid = _dot(x, wgu_bf_ref[...]) + bgu_ref[...]
        gate = jnp.minimum(hid[:, :D_FF], SWIGLU_LIMIT)
        up = jnp.clip(hid[:, D_FF:], -SWIGLU_LIMIT, SWIGLU_LIMIT)
        act = gate * _sigmoid(SWIGLU_ALPHA * gate) * (up + 1.0)
        y = _dot(act.astype(BF16), wd_bf_ref[...]) + bd_ref[...]
        row = lax.broadcasted_iota(jnp.int32, (tm, 1), 0)
        mine = jnp.logical_and(row >= lo, row < hi)

        @pl.when(lo == 0)
        def _():
            ys_ref[...] = jnp.where(mine, y, 0.0)

        @pl.when(lo > 0)
        def _():
            ys_ref[...] = jnp.where(mine, y, ys_ref[...])


def _expert_call(item_tile, item_exp, item_lo, item_hi, xs, wgu, bgu, wd, bd):
    m, w = xs.shape
    tm = EXPERT_TILE
    n_items = item_tile.shape[0]
    grid_spec = pltpu.PrefetchScalarGridSpec(
        num_scalar_prefetch=4,
        grid=(n_items,),
        in_specs=[pl.BlockSpec((tm, w), lambda i, t, e, lo, hi: (t[i], 0)),
                  pl.BlockSpec((None, D_MODEL, 2 * D_FF), lambda i, t, e, lo, hi: (e[i], 0, 0)),
                  pl.BlockSpec((None, 1, 2 * D_FF), lambda i, t, e, lo, hi: (e[i], 0, 0)),
                  pl.BlockSpec((None, D_FF, D_MODEL), lambda i, t, e, lo, hi: (e[i], 0, 0)),
                  pl.BlockSpec((None, 1, D_MODEL), lambda i, t, e, lo, hi: (e[i], 0, 0))],
        out_specs=pl.BlockSpec((tm, D_MODEL), lambda i, t, e, lo, hi: (t[i], 0)),
        scratch_shapes=[pltpu.VMEM((D_MODEL, 2 * D_FF), BF16), pltpu.VMEM((D_FF, D_MODEL), BF16)],
    )
    return pl.pallas_call(
        _expert_kernel,
        grid_spec=grid_spec,
        out_shape=jax.ShapeDtypeStruct((m, D_MODEL), F32),
        compiler_params=_cparams(("arbitrary",)),
    )(item_tile, item_exp, item_lo, item_hi, xs, wgu, bgu, wd, bd)


def _combine_kernel(dest_ref, ys_ref, h1_ref, rg_ref, g2_ref, b2_ref, o_ref, buf_ref, sem):
    tm = h1_ref.shape[0]

    def issue(r, carry):
        for k in range(TOP_K):
            d = dest_ref[0, r * TOP_K + k]
            _row_copy(ys_ref.at[pl.ds(d, 1), :], buf_ref.at[k, pl.ds(r, 1), :], sem).start()
        return carry

    lax.fori_loop(0, tm, issue, 0)

    def drain(r, carry):
        for k in range(TOP_K):
            d = dest_ref[0, r * TOP_K + k]
            _row_copy(ys_ref.at[pl.ds(d, 1), :], buf_ref.at[k, pl.ds(r, 1), :], sem).wait()
        return carry

    lax.fori_loop(0, tm, drain, 0)

    rg = rg_ref[...]
    ffn = rg[:, 0:1] * buf_ref[0]
    for k in range(1, TOP_K):
        ffn = ffn + rg[:, k:k + 1] * buf_ref[k]
    o_ref[...] = _layer_norm(DEEPNORM_ALPHA * h1_ref[...] + ffn, g2_ref[...], b2_ref[...])


def _combine_call(dest3, ys, h1, rg, g2, b2, tm):
    n, d = h1.shape
    rowf = pl.BlockSpec((tm, d), lambda i: (i, 0))
    return pl.pallas_call(
        _combine_kernel,
        grid=(n // tm,),
        in_specs=[pl.BlockSpec((None, 1, tm * TOP_K), lambda i: (i, 0, 0), memory_space=pltpu.SMEM),
                  pl.BlockSpec(memory_space=pl.ANY), rowf,
                  pl.BlockSpec((tm, LANES), lambda i: (i, 0)), _const_spec((1, d)), _const_spec((1, d))],
        out_specs=rowf,
        out_shape=jax.ShapeDtypeStruct((n, d), F32),
        scratch_shapes=[pltpu.VMEM((TOP_K, tm, d), F32), pltpu.SemaphoreType.DMA(())],
        compiler_params=_cparams(("arbitrary",)),
    )(dest3, ys, h1, rg, g2, b2)


def _work_items(counts, m):
    tm = EXPERT_TILE
    n_tiles = m // tm
    max_items = n_tiles + N_EXPERTS - 1
    grp_end = jnp.cumsum(counts)
    grp_start = grp_end - counts
    first_tile = grp_start // tm
    last_tile = (grp_end - 1) // tm
    n_e = jnp.where(counts > 0, last_tile - first_tile + 1, 0)
    item_end = jnp.cumsum(n_e)
    item_start = item_end - n_e
    total = item_end[-1]
    i = jnp.arange(max_items, dtype=jnp.int32)
    valid = i < total
    ic = jnp.minimum(i, total - 1)
    e = jnp.minimum(jnp.sum(item_end[None, :] <= ic[:, None], axis=1), N_EXPERTS - 1).astype(jnp.int32)
    tile = first_tile[e] + (ic - item_start[e])
    lo = jnp.maximum(grp_start[e], tile * tm) - tile * tm
    hi = jnp.minimum(grp_end[e], (tile + 1) * tm) - tile * tm
    zero = jnp.zeros_like(lo)
    return (tile.astype(jnp.int32), e, jnp.where(valid, lo, zero).astype(jnp.int32),
            jnp.where(valid, hi, zero).astype(jnp.int32), grp_start)


def kernel(x, meta_tokens, ln_in_g, ln_in_b, w_in, b_gate, conv_w, conv_b, dt_bias, a_log, d_skip, ssd_norm_w, w_ssd_out, lam_q1, lam_k1, lam_q2, lam_k2, subln_w, w_da_out, w_out, ln1_g, ln1_b, w_router, b_router, w_gate_up, b_gate_up, w_down, b_down, ln2_g, ln2_b):
    bsz, s, d = x.shape
    n = bsz * s
    l = 0
    row = lambda v: v.reshape(1, -1).astype(F32)

    w = w_in[l]
    c0 = SSD_INNER
    c1 = c0 + SSD_CONV_DIM
    c2 = c1 + SSD_HEADS
    c3 = c2 + 3 * D_MODEL
    w_z = w[:, :c0].astype(BF16)
    w_xbc = w[:, c0:c1].astype(BF16)
    w_dt = jnp.pad(w[:, c1:c2], ((0, 0), (0, LANES - SSD_HEADS))).astype(BF16)
    w_qkv = w[:, c2:c3].astype(BF16)
    w_g = w[:, c3:].astype(BF16)
    g0, b0 = row(ln_in_g), row(ln_in_b)

    x2d = x.reshape(n, d)
    q, k, v = _qkv_call(x2d, g0, b0, w_qkv, ROW_TILE)
    _, km, vm = _qkv_call(meta_tokens.astype(F32), g0, b0, w_qkv, N_META)
    km = jnp.pad(km, ((0, LANES - N_META), (0, 0)))
    vm = jnp.pad(vm, ((0, LANES - N_META), (0, 0)))

    lam = (jnp.exp(jnp.sum(lam_q1[l].astype(F32) * lam_k1[l].astype(F32)))
           - jnp.exp(jnp.sum(lam_q2[l].astype(F32) * lam_k2[l].astype(F32))) + LAMBDA_INIT)
    da_n = _attn_call(lam.reshape(1, 1), q.reshape(bsz, s, d), k.reshape(bsz, s, d),
                      v.reshape(bsz, s, d), km, vm, row(subln_w[l]))

    meta_tile = jnp.pad(meta_tokens.astype(F32), ((SSD_TILE - N_META, 0), (0, 0)))
    pad_h = lambda vec: jnp.pad(row(vec), ((0, 0), (0, LANES - SSD_HEADS)))
    ssd_n = _ssd_call(x, meta_tile, g0, b0, w_z, w_xbc, w_dt, conv_w[l].astype(F32), row(conv_b[l]),
                      pad_h(dt_bias[l]), pad_h(a_log[l]),
                      row(jnp.repeat(d_skip[l].astype(F32), SSD_HEAD_DIM)), row(ssd_norm_w[l]))

    w_r = jnp.pad(w_router[l], ((0, 0), (0, LANES - N_EXPERTS))).astype(BF16)
    b_r = jnp.pad(row(b_router[l]), ((0, 0), (0, LANES - N_EXPERTS)))
    h1, hp, ri, rg, cnt = _merge_call(
        x2d, ssd_n.reshape(n, d), da_n.reshape(n, d), g0, b0, w_g, row(b_gate[l]),
        w_ssd_out[l].astype(BF16), w_da_out[l].astype(BF16), w_out[l].astype(BF16),
        row(ln1_g[l]), row(ln1_b[l]), w_r, b_r, ROW_TILE)

    counts = cnt[0, :N_EXPERTS].astype(jnp.int32)
    m = n * TOP_K
    item_tile, item_exp, item_lo, item_hi, grp_start = _work_items(counts, m)
    e_ids = jnp.arange(N_EXPERTS, dtype=jnp.int32)
    start_of = jnp.sum(jnp.where(ri[:, :TOP_K, None] == e_ids, grp_start.astype(jnp.int32), 0), axis=-1)
    dest = start_of + ri[:, TOP_K:2 * TOP_K]
    dest3 = dest.astype(jnp.int32).reshape(n // MOVE_TILE, 1, MOVE_TILE * TOP_K)

    xs = _dispatch_call(dest3, hp, MOVE_TILE)
    ys = _expert_call(item_tile, item_exp, item_lo, item_hi, xs,
                      w_gate_up[l].astype(F32), b_gate_up[l].reshape(N_EXPERTS, 1, -1).astype(F32),
                      w_down[l].astype(F32), b_down[l].reshape(N_EXPERTS, 1, -1).astype(F32))
    out = _combine_call(dest3, ys, h1, rg, row(ln2_g[l]), row(ln2_b[l]), MOVE_TILE)
    return out.reshape(bsz, s, d)
```

```python
import functools
import math

import jax
import jax.numpy as jnp
from jax import lax
from jax.experimental import pallas as pl
from jax.experimental.pallas import tpu as pltpu

F32 = jnp.float32
BF16 = jnp.bfloat16

D_MODEL = 1024
N_META = 16
SSD_HEADS = 16
SSD_HEAD_DIM = 64
SSD_INNER = 1024
SSD_GROUPS = 4
SSD_STATE = 128
SSD_CONV = 4
SSD_CONV_DIM = 2048
DA_HEADS = 8
DA_HEAD_DIM = 64
N_EXPERTS = 32
TOP_K = 4
D_FF = 1024
SWIGLU_LIMIT = 7.0
SWIGLU_ALPHA = 1.702
DEPTH = 1
DEEPNORM_ALPHA = (2.0 * DEPTH) ** 0.25
LN_EPS = 1e-5
RMS_EPS = 1e-6
LAMBDA_INIT = 0.8 - 0.6 * math.exp(-0.3 * 0)
LOG2_E = math.log2(math.e)

LANES = 128
VMEM_LIMIT = 56 * 1024 * 1024

ROW_TILE = 512
ATT_TQ = 512
ATT_TK = 512
SSD_TILE = 256
SSD_CHUNK = 128
EXPERT_TILE = 512
SEG_ALIGN = 8
SEG_BITS = 7
SORT_ROWS = ROW_TILE * 4 + 32 * SEG_ALIGN
SORT_CHUNK = SORT_ROWS // 3
XS_WIDTH = 512 + 128
MOVE_BATCH = 32
ZERO_STEPS = 3


def _cparams(sem):
    return pltpu.CompilerParams(dimension_semantics=sem, vmem_limit_bytes=VMEM_LIMIT)


def _const_spec(shape):
    nd = len(shape)
    return pl.BlockSpec(shape, lambda *a: (0,) * nd)


def _layer_norm(x, g, b):
    mu = jnp.mean(x, axis=-1, keepdims=True)
    xc = x - mu
    var = jnp.mean(xc * xc, axis=-1, keepdims=True)
    return xc * lax.rsqrt(var + LN_EPS) * g + b


def _sigmoid(x):
    return 1.0 / (1.0 + jnp.exp(-x))


def _dot(a, b):
    return jnp.dot(a, b, preferred_element_type=F32)


def _dot_nt(a, b):
    return lax.dot_general(a, b, (((1,), (1,)), ((), ())), preferred_element_type=F32)


def _qkv_kernel(x_ref, g_ref, b_ref, w_ref, q_ref, k_ref, v_ref):
    h = _layer_norm(x_ref[...], g_ref[...], b_ref[...]).astype(BF16)
    acc = _dot(h, w_ref[...])
    d = D_MODEL
    q_ref[...] = (acc[:, :d] * (DA_HEAD_DIM ** -0.5 * LOG2_E)).astype(BF16)
    k_ref[...] = acc[:, d:2 * d].astype(BF16)
    v_ref[...] = acc[:, 2 * d:].astype(BF16)


def _qkv_call(x2d, g, b, w_qkv, tm):
    n = x2d.shape[0]
    out = jax.ShapeDtypeStruct((n, D_MODEL), BF16)
    row = pl.BlockSpec((tm, D_MODEL), lambda i: (i, 0))
    return pl.pallas_call(
        _qkv_kernel,
        grid=(n // tm,),
        in_specs=[row, _const_spec((1, D_MODEL)), _const_spec((1, D_MODEL)),
                  _const_spec((D_MODEL, 3 * D_MODEL))],
        out_specs=[row, row, row],
        out_shape=[out, out, out],
        compiler_params=_cparams(("parallel",)),
    )(x2d, g, b, w_qkv)


def _attn_kernel(lam_ref, q_ref, k_ref, v_ref, km_ref, vm_ref, sw_ref, o_ref,
                 sa_ref, sb_ref, m_ref, l_ref, acc_ref):
    tq, tk = ATT_TQ, ATT_TK
    i = pl.program_id(2)
    lane = lax.broadcasted_iota(jnp.int32, (1, LANES), 1)
    q = q_ref[...]
    zero = jnp.zeros_like(q)
    q2 = jnp.concatenate([jnp.where(lane < DA_HEAD_DIM, q, zero),
                          jnp.where(lane >= DA_HEAD_DIM, q, zero)], axis=0)

    def scores(j):
        off = pl.multiple_of(j * tk, tk)
        return _dot_nt(q2, k_ref[pl.ds(off, tk), :])

    def absorb(s, j):
        m_old = m_ref[...]
        m_new = jnp.maximum(m_old, jnp.max(s, axis=1, keepdims=True))
        alpha = jnp.exp2(m_old - m_new)
        p = jnp.exp2(s - jnp.concatenate([m_new] * (tk // LANES), axis=1))
        psum = p[:, :LANES]
        for c in range(1, tk // LANES):
            psum = psum + p[:, c * LANES:(c + 1) * LANES]
        l_ref[...] = alpha * l_ref[...] + psum
        off = pl.multiple_of(j * tk, tk)
        acc_ref[...] = alpha * acc_ref[...] + _dot(p.astype(BF16), v_ref[pl.ds(off, tk), :])
        m_ref[...] = m_new

    meta_ok = lax.broadcasted_iota(jnp.int32, (1, LANES), 1) < N_META
    sm = jnp.where(meta_ok, _dot_nt(q2, km_ref[...]), -jnp.inf)
    m0 = jnp.broadcast_to(jnp.max(sm, axis=1, keepdims=True), sm.shape)
    p0 = jnp.exp2(sm - m0)
    m_ref[...] = m0
    l_ref[...] = p0
    acc_ref[...] = _dot(p0.astype(BF16), vm_ref[...])

    sa_ref[...] = scores(0)

    def body(j, carry):
        @pl.when(j % 2 == 0)
        def _():
            nxt = scores(j + 1)
            absorb(sa_ref[...], j)
            sb_ref[...] = nxt

        @pl.when(j % 2 == 1)
        def _():
            nxt = scores(j + 1)
            absorb(sb_ref[...], j)
            sa_ref[...] = nxt

        return carry

    lax.fori_loop(0, i, body, 0)

    qc = (lax.broadcasted_iota(jnp.int32, (2 * tq, tk), 0) % tq) // 64
    kc = lax.broadcasted_iota(jnp.int32, (2 * tq, tk), 1) // 64
    vis = kc <= qc

    @pl.when(i % 2 == 0)
    def _():
        absorb(jnp.where(vis, sa_ref[...], -jnp.inf), i)

    @pl.when(i % 2 == 1)
    def _():
        absorb(jnp.where(vis, sb_ref[...], -jnp.inf), i)

    a = acc_ref[...] / jnp.sum(l_ref[...], axis=1, keepdims=True)
    o = a[:tq] - lam_ref[0, 0] * a[tq:]
    o = o * lax.rsqrt(jnp.mean(o * o, axis=-1, keepdims=True) + RMS_EPS) * sw_ref[...]
    o_ref[...] = (o * (1.0 - LAMBDA_INIT)).astype(BF16)


def _attn_call(lam, q, k, v, km, vm, subln_w):
    bsz, s, _ = q.shape
    nq = s // ATT_TQ
    qspec = pl.BlockSpec((None, ATT_TQ, LANES), lambda b, h, i: (b, i, h))
    kvspec = pl.BlockSpec((None, s, LANES), lambda b, h, i: (b, 0, h))
    mspec = pl.BlockSpec((LANES, LANES), lambda b, h, i: (0, h))
    return pl.pallas_call(
        _attn_kernel,
        grid=(bsz, DA_HEADS, nq),
        in_specs=[pl.BlockSpec(memory_space=pltpu.SMEM), qspec, kvspec, kvspec, mspec, mspec,
                  _const_spec((1, LANES))],
        out_specs=qspec,
        out_shape=jax.ShapeDtypeStruct((bsz, s, D_MODEL), BF16),
        scratch_shapes=[pltpu.VMEM((2 * ATT_TQ, ATT_TK), F32), pltpu.VMEM((2 * ATT_TQ, ATT_TK), F32),
                        pltpu.VMEM((2 * ATT_TQ, LANES), F32), pltpu.VMEM((2 * ATT_TQ, LANES), F32),
                        pltpu.VMEM((2 * ATT_TQ, LANES), F32)],
        compiler_params=_cparams(("parallel", "parallel", "arbitrary")),
    )(lam, q, k, v, km, vm, subln_w)


def _split3(a):
    hi = a.astype(BF16)
    r = a - hi.astype(F32)
    mid = r.astype(BF16)
    lo = (r - mid.astype(F32)).astype(BF16)
    return hi, mid, lo


def _ssd_kernel(x_ref, meta_ref, g_ref, b_ref, wz_ref, wxbc_ref, wdt_ref, cw_ref, cb_ref,
                dtb_ref, alog_ref, dskip_ref, nw_ref, o_ref, state_ref, cbuf_ref, y_ref):
    lt, lc = SSD_TILE, SSD_CHUNK
    t = pl.program_id(1)
    is_meta = t == 0

    @pl.when(is_meta)
    def _():
        state_ref[...] = jnp.zeros_like(state_ref)
        cbuf_ref[0:8, :] = jnp.zeros((8, SSD_CONV_DIM), F32)

    row = lax.broadcasted_iota(jnp.int32, (lt, 1), 0)
    valid = jnp.logical_or(jnp.logical_not(is_meta), row >= lt - N_META)
    x = jnp.where(is_meta, meta_ref[...], x_ref[...])
    h = _layer_norm(x, g_ref[...], b_ref[...]).astype(BF16)
    z = _dot(h, wz_ref[...])
    xbc = jnp.where(valid, _dot(h, wxbc_ref[...]), 0.0)
    dtr = _dot(h, wdt_ref[...])

    cbuf_ref[8:8 + lt, :] = xbc
    conv = (cw_ref[0:1, :] * cbuf_ref[5:5 + lt, :] + cw_ref[1:2, :] * cbuf_ref[6:6 + lt, :]
            + cw_ref[2:3, :] * cbuf_ref[7:7 + lt, :] + cw_ref[3:4, :] * xbc + cb_ref[...])
    cbuf_ref[0:8, :] = cbuf_ref[lt:lt + 8, :]
    act = jnp.where(valid, conv * _sigmoid(conv), 0.0)
    xs = act[:, :SSD_INNER]
    bm = act[:, SSD_INNER:SSD_INNER + SSD_GROUPS * SSD_STATE]
    cm = act[:, SSD_INNER + SSD_GROUPS * SSD_STATE:].astype(BF16)
    bm_t = bm.T.astype(BF16)

    dtv = dtr + dtb_ref[...]
    dt = jnp.maximum(dtv, 0.0) + jnp.log1p(jnp.exp(-jnp.abs(dtv)))
    dt = jnp.where(valid, dt, 0.0)
    a = dt * (-jnp.exp(alog_ref[...]))

    lane = lax.broadcasted_iota(jnp.int32, (1, LANES), 1)
    left = lane < SSD_HEAD_DIM
    tri_r = lax.broadcasted_iota(jnp.int32, (lc, lc), 0)
    tri_c = lax.broadcasted_iota(jnp.int32, (lc, lc), 1)
    causal = tri_c <= tri_r
    tri = jnp.where(causal, 1.0, 0.0).astype(BF16)

    for c in range(lt // lc):
        rs = slice(c * lc, (c + 1) * lc)
        hi, mid, lo = _split3(a[rs])
        acs = _dot(tri, hi) + _dot(tri, mid) + _dot(tri, lo)
        acs_t = acs.T
        dt_c = dt[rs]
        for g in range(SSD_GROUPS):
            c_g = cm[rs, g * SSD_STATE:(g + 1) * SSD_STATE]
            bt_g = bm_t[g * SSD_STATE:(g + 1) * SSD_STATE, rs]
            cb = _dot(c_g, bt_g)
            for jj in range(2):
                j = 2 * g + jj
                h0, h1 = 2 * j, 2 * j + 1
                col0, col1 = acs[:, h0:h0 + 1], acs[:, h1:h1 + 1]
                row0, row1 = acs_t[h0:h0 + 1, :], acs_t[h1:h1 + 1, :]
                l0 = jnp.exp(jnp.where(causal, col0 - row0, -jnp.inf))
                l1 = jnp.exp(jnp.where(causal, col1 - row1, -jnp.inf))
                mm = jnp.concatenate([(cb * l0).astype(BF16), (cb * l1).astype(BF16)], axis=0)
                xp = xs[rs, j * LANES:(j + 1) * LANES]
                xdt = xp * jnp.where(left, dt_c[:, h0:h0 + 1], dt_c[:, h1:h1 + 1])
                yy = _dot(mm, xdt.astype(BF16))
                y_diag = jnp.where(left, yy[:lc], yy[lc:])
                acs_p = jnp.where(left, col0, col1)
                st = state_ref[j]
                y_off = _dot(c_g, st.astype(BF16)) * jnp.exp(acs_p)
                last_p = jnp.where(left, acs[lc - 1:lc, h0:h0 + 1], acs[lc - 1:lc, h1:h1 + 1])
                xd = (xdt * jnp.exp(last_p - acs_p)).astype(BF16)
                state_ref[j] = st * jnp.exp(last_p) + _dot(bt_g, xd)
                y_ref[rs, j * LANES:(j + 1) * LANES] = (
                    y_diag + y_off + xp * dskip_ref[:, j * LANES:(j + 1) * LANES])

    gy = y_ref[...] * (z * _sigmoid(z))
    gw = SSD_INNER // SSD_GROUPS
    outs = []
    for g in range(SSD_GROUPS):
        gg = gy[:, g * gw:(g + 1) * gw]
        outs.append(gg * lax.rsqrt(jnp.mean(gg * gg, axis=-1, keepdims=True) + RMS_EPS))
    o_ref[...] = (jnp.concatenate(outs, axis=1) * nw_ref[...]).astype(BF16)


def _ssd_call(x3, meta_tile, g, b, wz, wxbc, wdt, cw, cb, dtb, alog, dskip, nw):
    bsz, s, _ = x3.shape
    lt = SSD_TILE
    nt = s // lt
    xspec = pl.BlockSpec((None, lt, D_MODEL), lambda bb, t: (bb, jnp.maximum(t - 1, 0), 0))
    return pl.pallas_call(
        _ssd_kernel,
        grid=(bsz, nt + 1),
        in_specs=[xspec, _const_spec((lt, D_MODEL)), _const_spec((1, D_MODEL)),
                  _const_spec((1, D_MODEL)), _const_spec((D_MODEL, SSD_INNER)),
                  _const_spec((D_MODEL, SSD_CONV_DIM)), _const_spec((D_MODEL, LANES)),
                  _const_spec((SSD_CONV, SSD_CONV_DIM)), _const_spec((1, SSD_CONV_DIM)),
                  _const_spec((1, LANES)), _const_spec((1, LANES)), _const_spec((1, SSD_INNER)),
                  _const_spec((1, SSD_INNER))],
        out_specs=xspec,
        out_shape=jax.ShapeDtypeStruct((bsz, s, SSD_INNER), BF16),
        scratch_shapes=[pltpu.VMEM((SSD_HEADS // 2, SSD_STATE, LANES), F32),
                        pltpu.VMEM((lt + 8, SSD_CONV_DIM), F32),
                        pltpu.VMEM((lt, SSD_INNER), F32)],
        compiler_params=_cparams(("parallel", "arbitrary")),
    )(x3, meta_tile, g, b, wz, wxbc, wdt, cw, cb, dtb, alog, dskip, nw)


def _merge_kernel(x_ref, ssd_ref, da_ref, g0_ref, b0_ref, wg_ref, bg_ref, wso_ref, wdo_ref, wo_ref,
                  g1_ref, b1_ref, wr_ref, br_ref, h1_ref, hb_ref, lp_ref, tab_ref):
    tm = x_ref.shape[0]
    d = D_MODEL

    h = _layer_norm(x_ref[...], g0_ref[...], b0_ref[...])
    gates = _sigmoid(_dot(h.astype(BF16), wg_ref[...]) + bg_ref[...])
    y_ssd = _dot(ssd_ref[...], wso_ref[...])
    y_da = _dot(da_ref[...], wdo_ref[...])
    merged = gates[:, :d] * y_ssd + gates[:, d:] * y_da
    mix = _dot(merged.astype(BF16), wo_ref[...])
    h1 = _layer_norm(DEEPNORM_ALPHA * h + mix, g1_ref[...], b1_ref[...])
    h1_ref[...] = h1
    hb = h1.astype(BF16)
    hb_ref[...] = hb

    lane = lax.broadcasted_iota(jnp.int32, (tm, LANES), 1)
    logits = _dot(hb, wr_ref[...]) + br_ref[...]
    logits = jnp.where(lane < N_EXPERTS, logits, -jnp.inf)
    vals, sels = [], []
    for k in range(TOP_K):
        mx = jnp.max(logits, axis=1, keepdims=True)
        idx = jnp.min(jnp.where(logits == mx, lane, LANES), axis=1, keepdims=True)
        sel = lane == idx
        logits = jnp.where(sel, -jnp.inf, logits)
        vals.append(mx)
        sels.append(sel)
    exps = [jnp.exp(v - vals[0]) for v in vals]
    den = exps[0] + exps[1] + exps[2] + exps[3]

    onehot = jnp.where(sels[0] | sels[1] | sels[2] | sels[3], 1.0, 0.0)
    r_i = lax.broadcasted_iota(jnp.int32, (tm, tm), 0)
    c_i = lax.broadcasted_iota(jnp.int32, (tm, tm), 1)
    strict = jnp.where(c_i < r_i, 1.0, 0.0).astype(BF16)
    before = _dot(strict, onehot.astype(BF16))
    cnt = jnp.sum(onehot, axis=0, keepdims=True)
    cnt_al = jnp.floor((cnt + (SEG_ALIGN - 1)) * (1.0 / SEG_ALIGN)) * SEG_ALIGN
    e_r = lax.broadcasted_iota(jnp.int32, (LANES, LANES), 0)
    e_c = lax.broadcasted_iota(jnp.int32, (LANES, LANES), 1)
    upper = jnp.where(e_r < e_c, 1.0, 0.0).astype(BF16)
    start = _dot(jnp.broadcast_to(cnt_al, (8, LANES)).astype(BF16), upper)[0:1, :]
    pos = before + start
    lp = jnp.zeros((tm, LANES), F32)
    for k in range(TOP_K):
        lp = jnp.where(lane == k, jnp.sum(jnp.where(sels[k], pos, 0.0), axis=1, keepdims=True), lp)
        lp = jnp.where(lane == TOP_K + k, exps[k] / den, lp)
    lp_ref[...] = lp
    row8 = lax.broadcasted_iota(jnp.int32, (8, LANES), 0)
    tab_ref[...] = jnp.where(row8 == 0, cnt, jnp.where(row8 == 1, cnt_al, jnp.where(row8 == 2, start, 0.0)))


def _merge_call(x2d, ssd_n, da_n, g0, b0, wg, bg, wso, wdo, wo, g1, b1, wr, br, tm):
    n = x2d.shape[0]
    d = D_MODEL
    rowf = pl.BlockSpec((tm, d), lambda i: (i, 0))
    rowl = pl.BlockSpec((tm, LANES), lambda i: (i, 0))
    return pl.pallas_call(
        _merge_kernel,
        grid=(n // tm,),
        in_specs=[rowf, rowf, rowf, _const_spec((1, d)), _const_spec((1, d)),
                  _const_spec((d, 2 * d)), _const_spec((1, 2 * d)), _const_spec((d, d)),
                  _const_spec((d, d)), _const_spec((d, d)), _const_spec((1, d)), _const_spec((1, d)),
                  _const_spec((d, LANES)), _const_spec((1, LANES))],
        out_specs=[rowf, rowf, rowl, pl.BlockSpec((8, LANES), lambda i: (i, 0))],
        out_shape=[jax.ShapeDtypeStruct((n, d), F32), jax.ShapeDtypeStruct((n, d), BF16),
                   jax.ShapeDtypeStruct((n, LANES), F32),
                   jax.ShapeDtypeStruct((n // tm * 8, LANES), F32)],
        compiler_params=_cparams(("parallel",)),
    )(x2d, ssd_n, da_n, g0, b0, wg, bg, wso, wdo, wo, g1, b1, wr, br)


def _hits(lp_t, k, rows, r0):
    r = lax.broadcasted_iota(jnp.int32, (rows, lp_t.shape[1]), 0) + r0
    return r == lp_t[k:k + 1, :].astype(jnp.int32)


def _sort_kernel(hb_ref, lp_ref, xs_ref):
    d = D_MODEL
    lp_t = lp_ref[...].T
    hb = hb_ref[...]
    for c in range(SORT_ROWS // SORT_CHUNK):
        r0 = c * SORT_CHUNK
        hits = [_hits(lp_t, k, SORT_CHUNK, r0) for k in range(TOP_K)]
        sel = jnp.where(hits[0] | hits[1] | hits[2] | hits[3], 1.0, 0.0).astype(BF16)
        xsort = _dot(sel, hb)
        bits = pltpu.bitcast(xsort, jnp.uint32)
        xs_ref[r0:r0 + SORT_CHUNK, :d // 2] = bits[:, :d // 2] | (bits[:, d // 2:] >> 16)
        g = jnp.where(hits[0], lp_t[TOP_K:TOP_K + 1, :], 0.0)
        for k in range(1, TOP_K):
            g = g + jnp.where(hits[k], lp_t[TOP_K + k:TOP_K + k + 1, :], 0.0)
        gsum = jnp.sum(g, axis=1, keepdims=True)
        xs_ref[r0:r0 + SORT_CHUNK, d // 2:] = pltpu.bitcast(
            jnp.broadcast_to(gsum, (SORT_CHUNK, LANES)), jnp.uint32)


def _sort_call(hb, lp, tm):
    n, d = hb.shape
    return pl.pallas_call(
        _sort_kernel,
        grid=(n // tm,),
        in_specs=[pl.BlockSpec((tm, d), lambda i: (i, 0)), pl.BlockSpec((tm, LANES), lambda i: (i, 0))],
        out_specs=pl.BlockSpec((SORT_ROWS, XS_WIDTH), lambda i: (i, 0)),
        out_shape=jax.ShapeDtypeStruct((n // tm * SORT_ROWS, XS_WIDTH), jnp.uint32),
        compiler_params=_cparams(("parallel",)),
    )(hb, lp)


def _segment_copy(src_ref, dst_ref, s_al, d_al, n_al, sem, wait):
    for b in reversed(range(SEG_BITS)):
        size = (1 << b) * SEG_ALIGN
        done = (n_al >> (b + 1)) << (b + 1)

        @pl.when((n_al & (1 << b)) != 0)
        def _():
            s0 = pl.multiple_of((s_al + done) * SEG_ALIGN, SEG_ALIGN)
            d0 = pl.multiple_of((d_al + done) * SEG_ALIGN, SEG_ALIGN)
            cp = pltpu.make_async_copy(src_ref.at[pl.ds(s0, size), :], dst_ref.at[pl.ds(d0, size), :], sem)
            if wait:
                cp.wait()
            else:
                cp.start()


def _move_kernel(src_tab, dst_tab, len_tab, a_ref, z_ref, o_ref, sems):
    i = pl.program_id(0)
    n_steps = pl.num_programs(0)
    n_data = n_steps - ZERO_STEPS

    def batch(step, wait):
        sem = sems.at[step % 2]

        def run_all(from_ref):
            def one(j, carry):
                s = step * MOVE_BATCH + j
                _segment_copy(from_ref, o_ref, src_tab[s], dst_tab[s], len_tab[s], sem, wait)
                return carry

            lax.fori_loop(0, MOVE_BATCH, one, 0)

        @pl.when(step < n_data)
        def _():
            run_all(a_ref)

        @pl.when(step >= n_data)
        def _():
            run_all(z_ref)

    batch(i, False)

    @pl.when(i > 0)
    def _():
        batch(i - 1, True)

    @pl.when(i == n_steps - 1)
    def _():
        batch(i, True)


def _move_call(src_tab, dst_tab, len_tab, a, z, out_rows):
    n_seg = src_tab.shape[0]
    grid_spec = pltpu.PrefetchScalarGridSpec(
        num_scalar_prefetch=3,
        grid=(n_seg // MOVE_BATCH,),
        in_specs=[pl.BlockSpec(memory_space=pl.ANY), pl.BlockSpec(memory_space=pl.ANY)],
        out_specs=pl.BlockSpec(memory_space=pl.ANY),
        scratch_shapes=[pltpu.SemaphoreType.DMA((2,))],
    )
    return pl.pallas_call(
        _move_kernel,
        grid_spec=grid_spec,
        out_shape=jax.ShapeDtypeStruct((out_rows, a.shape[1]), a.dtype),
        compiler_params=_cparams(("arbitrary",)),
    )(src_tab, dst_tab, len_tab, a, z)


def _expert_kernel(tile_ref, exp_ref, lo_ref, hi_ref, xs_ref, wgu_ref, bgu_ref, wd_ref, bd_ref, ys_ref,
                   wgu_bf_ref, wd_bf_ref):
    i = pl.program_id(0)
    lo = lo_ref[i]
    hi = hi_ref[i]
    tm = xs_ref.shape[0]
    d = D_MODEL

    @pl.when(jnp.logical_or(i == 0, exp_ref[i] != exp_ref[jnp.maximum(i - 1, 0)]))
    def _():
        wgu_bf_ref[...] = wgu_ref[...].astype(BF16)
        wd_bf_ref[...] = wd_ref[...].astype(BF16)

    @pl.when(lo > hi)
    def _():
        ys_ref[...] = jnp.zeros_like(ys_ref)

    @pl.when(lo < hi)
    def _():
        w = xs_ref[:, :d // 2]
        x_hi = pltpu.bitcast(w & jnp.uint32(0xFFFF0000), F32).astype(BF16)
        x_lo = pltpu.bitcast(w << 16, F32).astype(BF16)
        x = jnp.concatenate([x_hi, x_lo], axis=1)
        hid = _dot(x, wgu_bf_ref[...]) + bgu_ref[...]
        gate = jnp.minimum(hid[:, :D_FF], SWIGLU_LIMIT)
        up = jnp.clip(hid[:, D_FF:], -SWIGLU_LIMIT, SWIGLU_LIMIT)
        act = gate * _sigmoid(SWIGLU_ALPHA * gate) * (up + 1.0)
        y = _dot(act.astype(BF16), wd_bf_ref[...]) + bd_ref[...]
        rgate = pltpu.bitcast(xs_ref[:, d // 2:], F32)
        y = y * jnp.concatenate([rgate] * (d // LANES), axis=1)
        row = lax.broadcasted_iota(jnp.int32, (tm, 1), 0)
        mine = jnp.logical_and(row >= lo, row < hi)

        @pl.when(lo == 0)
        def _():
            ys_ref[...] = jnp.where(mine, y, 0.0)

        @pl.when(lo > 0)
        def _():
            ys_ref[...] = jnp.where(mine, y, ys_ref[...])


def _expert_call(item_tile, item_exp, item_lo, item_hi, xs, wgu, bgu, wd, bd):
    m, w = xs.shape
    tm = EXPERT_TILE
    n_items = item_tile.shape[0]
    grid_spec = pltpu.PrefetchScalarGridSpec(
        num_scalar_prefetch=4,
        grid=(n_items,),
        in_specs=[pl.BlockSpec((tm, w), lambda i, t, e, lo, hi: (t[i], 0)),
                  pl.BlockSpec((None, D_MODEL, 2 * D_FF), lambda i, t, e, lo, hi: (e[i], 0, 0)),
                  pl.BlockSpec((None, 1, 2 * D_FF), lambda i, t, e, lo, hi: (e[i], 0, 0)),
                  pl.BlockSpec((None, D_FF, D_MODEL), lambda i, t, e, lo, hi: (e[i], 0, 0)),
                  pl.BlockSpec((None, 1, D_MODEL), lambda i, t, e, lo, hi: (e[i], 0, 0))],
        out_specs=pl.BlockSpec((tm, D_MODEL), lambda i, t, e, lo, hi: (t[i], 0)),
        scratch_shapes=[pltpu.VMEM((D_MODEL, 2 * D_FF), BF16), pltpu.VMEM((D_FF, D_MODEL), BF16)],
    )
    return pl.pallas_call(
        _expert_kernel,
        grid_spec=grid_spec,
        out_shape=jax.ShapeDtypeStruct((m, D_MODEL), F32),
        compiler_params=_cparams(("arbitrary",)),
    )(item_tile, item_exp, item_lo, item_hi, xs, wgu, bgu, wd, bd)


def _combine_kernel(ys_ref, lp_ref, h1_ref, g2_ref, b2_ref, o_ref):
    tm = h1_ref.shape[0]
    lp = lp_ref[...]
    ffn = jnp.zeros((tm, D_MODEL), F32)
    for c in range(SORT_ROWS // SORT_CHUNK):
        r0 = c * SORT_CHUNK
        r = lax.broadcasted_iota(jnp.int32, (tm, SORT_CHUNK), 1) + r0
        hit = r == lp[:, 0:1].astype(jnp.int32)
        for k in range(1, TOP_K):
            hit = hit | (r == lp[:, k:k + 1].astype(jnp.int32))
        sel = jnp.where(hit, 1.0, 0.0).astype(BF16)
        y = ys_ref[r0:r0 + SORT_CHUNK, :]
        y_hi = y.astype(BF16)
        y_lo = (y - y_hi.astype(F32)).astype(BF16)
        ffn = ffn + _dot(sel, y_hi) + _dot(sel, y_lo)
    o_ref[...] = _layer_norm(DEEPNORM_ALPHA * h1_ref[...] + ffn, g2_ref[...], b2_ref[...])


def _combine_call(ysl, lp, h1, g2, b2, tm):
    n, d = h1.shape
    rowf = pl.BlockSpec((tm, d), lambda i: (i, 0))
    return pl.pallas_call(
        _combine_kernel,
        grid=(n // tm,),
        in_specs=[pl.BlockSpec((SORT_ROWS, d), lambda i: (i, 0)), pl.BlockSpec((tm, LANES), lambda i: (i, 0)),
                  rowf, _const_spec((1, d)), _const_spec((1, d))],
        out_specs=rowf,
        out_shape=jax.ShapeDtypeStruct((n, d), F32),
        compiler_params=_cparams(("parallel",)),
    )(ysl, lp, h1, g2, b2)


def _work_items(counts, m):
    tm = EXPERT_TILE
    n_tiles = m // tm
    max_items = n_tiles + N_EXPERTS - 1
    grp_end = jnp.cumsum(counts)
    grp_start = grp_end - counts
    first_tile = grp_start // tm
    last_tile = (grp_end - 1) // tm
    n_e = jnp.where(counts > 0, last_tile - first_tile + 1, 0)
    item_end = jnp.cumsum(n_e)
    item_start = item_end - n_e
    total = item_end[-1]
    i = jnp.arange(max_items, dtype=jnp.int32)
    valid = i < total
    ic = jnp.minimum(i, total - 1)
    e = jnp.minimum(jnp.sum(item_end[None, :] <= ic[:, None], axis=1), N_EXPERTS - 1).astype(jnp.int32)
    tile = first_tile[e] + (ic - item_start[e])
    lo = jnp.maximum(grp_start[e], tile * tm) - tile * tm
    hi = jnp.minimum(grp_end[e], (tile + 1) * tm) - tile * tm
    used_tiles = (grp_end[-1] + tm - 1) // tm
    fill_tile = used_tiles + (i - total)
    is_fill = jnp.logical_and(jnp.logical_not(valid), fill_tile < n_tiles)
    tile = jnp.where(valid, tile, jnp.where(is_fill, fill_tile, n_tiles - 1))
    lo = jnp.where(valid, lo, jnp.where(is_fill, 1, 0))
    hi = jnp.where(valid, hi, 0)
    return tile.astype(jnp.int32), e, lo.astype(jnp.int32), hi.astype(jnp.int32), grp_start


def _pad_table(v, n):
    return jnp.concatenate([v.astype(jnp.int32), jnp.zeros((n - v.shape[0],), jnp.int32)])


def _segment_tables(tab, n_tiles):
    al = SEG_ALIGN
    t3 = tab.reshape(n_tiles, 8, LANES)
    cnt_al = t3[:, 1, :N_EXPERTS].astype(jnp.int32)
    start = t3[:, 2, :N_EXPERTS].astype(jnp.int32)
    grp = jnp.sum(cnt_al, axis=0)
    grp_start = jnp.cumsum(grp) - grp
    ahead = jnp.cumsum(cnt_al, axis=0) - cnt_al
    local = (jnp.arange(n_tiles, dtype=jnp.int32)[:, None] * SORT_ROWS + start).reshape(-1) // al
    glob = (grp_start[None, :] + ahead).reshape(-1) // al
    length = cnt_al.reshape(-1) // al
    used = jnp.sum(cnt_al, axis=1)
    total = jnp.sum(grp)
    return grp, local, glob, length, used, total


def kernel(x, meta_tokens, ln_in_g, ln_in_b, w_in, b_gate, conv_w, conv_b, dt_bias, a_log, d_skip, ssd_norm_w, w_ssd_out, lam_q1, lam_k1, lam_q2, lam_k2, subln_w, w_da_out, w_out, ln1_g, ln1_b, w_router, b_router, w_gate_up, b_gate_up, w_down, b_down, ln2_g, ln2_b):
    bsz, s, d = x.shape
    n = bsz * s
    l = 0
    row = lambda v: v.reshape(1, -1).astype(F32)

    w = w_in[l]
    c0 = SSD_INNER
    c1 = c0 + SSD_CONV_DIM
    c2 = c1 + SSD_HEADS
    c3 = c2 + 3 * D_MODEL
    w_z = w[:, :c0].astype(BF16)
    w_xbc = w[:, c0:c1].astype(BF16)
    w_dt = jnp.pad(w[:, c1:c2], ((0, 0), (0, LANES - SSD_HEADS))).astype(BF16)
    w_qkv = w[:, c2:c3].astype(BF16)
    w_g = w[:, c3:].astype(BF16)
    g0, b0 = row(ln_in_g), row(ln_in_b)

    x2d = x.reshape(n, d)
    q, k, v = _qkv_call(x2d, g0, b0, w_qkv, ROW_TILE)
    _, km, vm = _qkv_call(meta_tokens.astype(F32), g0, b0, w_qkv, N_META)
    km = jnp.pad(km, ((0, LANES - N_META), (0, 0)))
    vm = jnp.pad(vm, ((0, LANES - N_META), (0, 0)))

    lam = (jnp.exp(jnp.sum(lam_q1[l].astype(F32) * lam_k1[l].astype(F32)))
           - jnp.exp(jnp.sum(lam_q2[l].astype(F32) * lam_k2[l].astype(F32))) + LAMBDA_INIT)
    da_n = _attn_call(lam.reshape(1, 1), q.reshape(bsz, s, d), k.reshape(bsz, s, d),
                      v.reshape(bsz, s, d), km, vm, row(subln_w[l]))

    meta_tile = jnp.pad(meta_tokens.astype(F32), ((SSD_TILE - N_META, 0), (0, 0)))
    pad_h = lambda vec: jnp.pad(row(vec), ((0, 0), (0, LANES - SSD_HEADS)))
    ssd_n = _ssd_call(x, meta_tile, g0, b0, w_z, w_xbc, w_dt, conv_w[l].astype(F32), row(conv_b[l]),
                      pad_h(dt_bias[l]), pad_h(a_log[l]),
                      row(jnp.repeat(d_skip[l].astype(F32), SSD_HEAD_DIM)), row(ssd_norm_w[l]))

    w_r = jnp.pad(w_router[l], ((0, 0), (0, LANES - N_EXPERTS))).astype(BF16)
    b_r = jnp.pad(row(b_router[l]), ((0, 0), (0, LANES - N_EXPERTS)))
    h1, hb, lp, tab = _merge_call(
        x2d, ssd_n.reshape(n, d), da_n.reshape(n, d), g0, b0, w_g, row(b_gate[l]),
        w_ssd_out[l].astype(BF16), w_da_out[l].astype(BF16), w_out[l].astype(BF16),
        row(ln1_g[l]), row(ln1_b[l]), w_r, b_r, ROW_TILE)
    xs_local = _sort_call(hb, lp, ROW_TILE)

    n_tiles = n // ROW_TILE
    n_pairs = n_tiles * N_EXPERTS
    m_rows = n * TOP_K + n_pairs * SEG_ALIGN
    m_rows = -(-m_rows // EXPERT_TILE) * EXPERT_TILE
    grp, local, glob, length, used, total = _segment_tables(tab, n_tiles)
    item_tile, item_exp, item_lo, item_hi, _ = _work_items(grp, m_rows)
    n_tab = n_pairs + ZERO_STEPS * MOVE_BATCH
    al = SEG_ALIGN

    tail_len = (-total) % EXPERT_TILE
    fill_dst = total + tail_len + jnp.arange(m_rows // EXPERT_TILE - n * TOP_K // EXPERT_TILE + 1,
                                             dtype=jnp.int32) * EXPERT_TILE
    fill_len = jnp.where(fill_dst + EXPERT_TILE <= m_rows, EXPERT_TILE, 0)
    src_t = _pad_table(local, n_tab)
    dst_t = _pad_table(jnp.concatenate([glob, (total // al)[None], fill_dst // al]), n_tab)
    len_t = _pad_table(jnp.concatenate([length, (tail_len // al)[None], fill_len // al]), n_tab)
    zeros_x = jnp.zeros((EXPERT_TILE, XS_WIDTH), jnp.uint32)
    xs = _move_call(src_t, dst_t, len_t, xs_local, zeros_x, m_rows)

    ys = _expert_call(item_tile, item_exp, item_lo, item_hi, xs,
                      w_gate_up[l].astype(F32), b_gate_up[l].reshape(N_EXPERTS, 1, -1).astype(F32),
                      w_down[l].astype(F32), b_down[l].reshape(N_EXPERTS, 1, -1).astype(F32))

    tile_base = jnp.arange(n_tiles, dtype=jnp.int32) * SORT_ROWS
    zdst = (tile_base + used) // al
    zlen = (SORT_ROWS - used) // al
    src_b = _pad_table(glob, n_tab)
    dst_b = _pad_table(jnp.concatenate([local, zdst]), n_tab)
    len_b = _pad_table(jnp.concatenate([length, zlen]), n_tab)
    zeros_y = jnp.zeros((EXPERT_TILE, d), F32)
    ysl = _move_call(src_b, dst_b, len_b, ys, zeros_y, n_tiles * SORT_ROWS)

    out = _combine_call(ysl, lp, h1, row(ln2_g[l]), row(ln2_b[l]), ROW_TILE)
    return out.reshape(bsz, s, d)
```

```python
import functools
import math

import jax
import jax.numpy as jnp
from jax import lax
from jax.experimental import pallas as pl
from jax.experimental.pallas import tpu as pltpu

F32 = jnp.float32
BF16 = jnp.bfloat16

D_MODEL = 1024
N_META = 16
SSD_HEADS = 16
SSD_HEAD_DIM = 64
SSD_INNER = 1024
SSD_GROUPS = 4
SSD_STATE = 128
SSD_CONV = 4
SSD_CONV_DIM = 2048
DA_HEADS = 8
DA_HEAD_DIM = 64
N_EXPERTS = 32
TOP_K = 4
D_FF = 1024
SWIGLU_LIMIT = 7.0
SWIGLU_ALPHA = 1.702
DEPTH = 1
DEEPNORM_ALPHA = (2.0 * DEPTH) ** 0.25
LN_EPS = 1e-5
RMS_EPS = 1e-6
LAMBDA_INIT = 0.8 - 0.6 * math.exp(-0.3 * 0)
LOG2_E = math.log2(math.e)

LANES = 128
VMEM_LIMIT = 56 * 1024 * 1024

ROW_TILE = 512
ATT_TQ = 512
ATT_TK = 512
SSD_TILE = 256
SSD_CHUNK = 128
EXPERT_TILE = 512
SEG_ALIGN = 8
SEG_BITS = 7
SORT_ROWS = ROW_TILE * 4 + 32 * SEG_ALIGN
SORT_CHUNK = SORT_ROWS // 3
XS_WIDTH = 512 + 128


def _cparams(sem):
    return pltpu.CompilerParams(dimension_semantics=sem, vmem_limit_bytes=VMEM_LIMIT)


def _const_spec(shape):
    nd = len(shape)
    return pl.BlockSpec(shape, lambda *a: (0,) * nd)


def _layer_norm(x, g, b):
    mu = jnp.mean(x, axis=-1, keepdims=True)
    xc = x - mu
    var = jnp.mean(xc * xc, axis=-1, keepdims=True)
    return xc * lax.rsqrt(var + LN_EPS) * g + b


def _sigmoid(x):
    return 1.0 / (1.0 + jnp.exp(-x))


def _dot(a, b):
    return jnp.dot(a, b, preferred_element_type=F32)


def _dot_nt(a, b):
    return lax.dot_general(a, b, (((1,), (1,)), ((), ())), preferred_element_type=F32)


def _qkv_kernel(x_ref, g_ref, b_ref, w_ref, q_ref, k_ref, v_ref):
    h = _layer_norm(x_ref[...], g_ref[...], b_ref[...]).astype(BF16)
    acc = _dot(h, w_ref[...])
    d = D_MODEL
    q_ref[...] = (acc[:, :d] * (DA_HEAD_DIM ** -0.5 * LOG2_E)).astype(BF16)
    k_ref[...] = acc[:, d:2 * d].astype(BF16)
    v_ref[...] = acc[:, 2 * d:].astype(BF16)


def _qkv_call(x2d, g, b, w_qkv, tm):
    n = x2d.shape[0]
    out = jax.ShapeDtypeStruct((n, D_MODEL), BF16)
    row = pl.BlockSpec((tm, D_MODEL), lambda i: (i, 0))
    return pl.pallas_call(
        _qkv_kernel,
        grid=(n // tm,),
        in_specs=[row, _const_spec((1, D_MODEL)), _const_spec((1, D_MODEL)),
                  _const_spec((D_MODEL, 3 * D_MODEL))],
        out_specs=[row, row, row],
        out_shape=[out, out, out],
        compiler_params=_cparams(("parallel",)),
    )(x2d, g, b, w_qkv)


def _attn_kernel(lam_ref, q_ref, k_ref, v_ref, km_ref, vm_ref, sw_ref, o_ref,
                 sa_ref, sb_ref, m_ref, l_ref, acc_ref):
    tq, tk = ATT_TQ, ATT_TK
    i = pl.program_id(2)
    lane = lax.broadcasted_iota(jnp.int32, (1, LANES), 1)
    q = q_ref[...]
    zero = jnp.zeros_like(q)
    q2 = jnp.concatenate([jnp.where(lane < DA_HEAD_DIM, q, zero),
                          jnp.where(lane >= DA_HEAD_DIM, q, zero)], axis=0)

    def scores(j):
        off = pl.multiple_of(j * tk, tk)
        return _dot_nt(q2, k_ref[pl.ds(off, tk), :])

    def absorb(s, j):
        m_old = m_ref[...]
        m_new = jnp.maximum(m_old, jnp.max(s, axis=1, keepdims=True))
        alpha = jnp.exp2(m_old - m_new)
        p = jnp.exp2(s - jnp.concatenate([m_new] * (tk // LANES), axis=1))
        psum = p[:, :LANES]
        for c in range(1, tk // LANES):
            psum = psum + p[:, c * LANES:(c + 1) * LANES]
        l_ref[...] = alpha * l_ref[...] + psum
        off = pl.multiple_of(j * tk, tk)
        acc_ref[...] = alpha * acc_ref[...] + _dot(p.astype(BF16), v_ref[pl.ds(off, tk), :])
        m_ref[...] = m_new

    meta_ok = lax.broadcasted_iota(jnp.int32, (1, LANES), 1) < N_META
    sm = jnp.where(meta_ok, _dot_nt(q2, km_ref[...]), -jnp.inf)
    m0 = jnp.broadcast_to(jnp.max(sm, axis=1, keepdims=True), sm.shape)
    p0 = jnp.exp2(sm - m0)
    m_ref[...] = m0
    l_ref[...] = p0
    acc_ref[...] = _dot(p0.astype(BF16), vm_ref[...])

    sa_ref[...] = scores(0)

    def body(j, carry):
        @pl.when(j % 2 == 0)
        def _():
            nxt = scores(j + 1)
            absorb(sa_ref[...], j)
            sb_ref[...] = nxt

        @pl.when(j % 2 == 1)
        def _():
            nxt = scores(j + 1)
            absorb(sb_ref[...], j)
            sa_ref[...] = nxt

        return carry

    lax.fori_loop(0, i, body, 0)

    qc = (lax.broadcasted_iota(jnp.int32, (2 * tq, tk), 0) % tq) // 64
    kc = lax.broadcasted_iota(jnp.int32, (2 * tq, tk), 1) // 64
    vis = kc <= qc

    @pl.when(i % 2 == 0)
    def _():
        absorb(jnp.where(vis, sa_ref[...], -jnp.inf), i)

    @pl.when(i % 2 == 1)
    def _():
        absorb(jnp.where(vis, sb_ref[...], -jnp.inf), i)

    a = acc_ref[...] / jnp.sum(l_ref[...], axis=1, keepdims=True)
    o = a[:tq] - lam_ref[0, 0] * a[tq:]
    o = o * lax.rsqrt(jnp.mean(o * o, axis=-1, keepdims=True) + RMS_EPS) * sw_ref[...]
    o_ref[...] = (o * (1.0 - LAMBDA_INIT)).astype(BF16)


def _attn_call(lam, q, k, v, km, vm, subln_w):
    bsz, s, _ = q.shape
    nq = s // ATT_TQ
    qspec = pl.BlockSpec((None, ATT_TQ, LANES), lambda b, h, i: (b, i, h))
    kvspec = pl.BlockSpec((None, s, LANES), lambda b, h, i: (b, 0, h))
    mspec = pl.BlockSpec((LANES, LANES), lambda b, h, i: (0, h))
    return pl.pallas_call(
        _attn_kernel,
        grid=(bsz, DA_HEADS, nq),
        in_specs=[pl.BlockSpec(memory_space=pltpu.SMEM), qspec, kvspec, kvspec, mspec, mspec,
                  _const_spec((1, LANES))],
        out_specs=qspec,
        out_shape=jax.ShapeDtypeStruct((bsz, s, D_MODEL), BF16),
        scratch_shapes=[pltpu.VMEM((2 * ATT_TQ, ATT_TK), F32), pltpu.VMEM((2 * ATT_TQ, ATT_TK), F32),
                        pltpu.VMEM((2 * ATT_TQ, LANES), F32), pltpu.VMEM((2 * ATT_TQ, LANES), F32),
                        pltpu.VMEM((2 * ATT_TQ, LANES), F32)],
        compiler_params=_cparams(("parallel", "parallel", "arbitrary")),
    )(lam, q, k, v, km, vm, subln_w)


def _split3(a):
    hi = a.astype(BF16)
    r = a - hi.astype(F32)
    mid = r.astype(BF16)
    lo = (r - mid.astype(F32)).astype(BF16)
    return hi, mid, lo


def _ssd_kernel(x_ref, meta_ref, g_ref, b_ref, wz_ref, wxbc_ref, wdt_ref, cw_ref, cb_ref,
                dtb_ref, alog_ref, dskip_ref, nw_ref, o_ref, state_ref, cbuf_ref, y_ref):
    lt, lc = SSD_TILE, SSD_CHUNK
    t = pl.program_id(1)
    is_meta = t == 0

    @pl.when(is_meta)
    def _():
        state_ref[...] = jnp.zeros_like(state_ref)
        cbuf_ref[0:8, :] = jnp.zeros((8, SSD_CONV_DIM), F32)

    row = lax.broadcasted_iota(jnp.int32, (lt, 1), 0)
    valid = jnp.logical_or(jnp.logical_not(is_meta), row >= lt - N_META)
    x = jnp.where(is_meta, meta_ref[...], x_ref[...])
    h = _layer_norm(x, g_ref[...], b_ref[...]).astype(BF16)
    z = _dot(h, wz_ref[...])
    xbc = jnp.where(valid, _dot(h, wxbc_ref[...]), 0.0)
    dtr = _dot(h, wdt_ref[...])

    cbuf_ref[8:8 + lt, :] = xbc
    conv = (cw_ref[0:1, :] * cbuf_ref[5:5 + lt, :] + cw_ref[1:2, :] * cbuf_ref[6:6 + lt, :]
            + cw_ref[2:3, :] * cbuf_ref[7:7 + lt, :] + cw_ref[3:4, :] * xbc + cb_ref[...])
    cbuf_ref[0:8, :] = cbuf_ref[lt:lt + 8, :]
    act = jnp.where(valid, conv * _sigmoid(conv), 0.0)
    xs = act[:, :SSD_INNER]
    bm = act[:, SSD_INNER:SSD_INNER + SSD_GROUPS * SSD_STATE]
    cm = act[:, SSD_INNER + SSD_GROUPS * SSD_STATE:].astype(BF16)
    bm_t = bm.T.astype(BF16)

    dtv = dtr + dtb_ref[...]
    dt = jnp.maximum(dtv, 0.0) + jnp.log1p(jnp.exp(-jnp.abs(dtv)))
    dt = jnp.where(valid, dt, 0.0)
    a = dt * (-jnp.exp(alog_ref[...]))

    lane = lax.broadcasted_iota(jnp.int32, (1, LANES), 1)
    left = lane < SSD_HEAD_DIM
    tri_r = lax.broadcasted_iota(jnp.int32, (lc, lc), 0)
    tri_c = lax.broadcasted_iota(jnp.int32, (lc, lc), 1)
    causal = tri_c <= tri_r
    tri = jnp.where(causal, 1.0, 0.0).astype(BF16)

    for c in range(lt // lc):
        rs = slice(c * lc, (c + 1) * lc)
        hi, mid, lo = _split3(a[rs])
        acs = _dot(tri, hi) + _dot(tri, mid) + _dot(tri, lo)
        acs_t = acs.T
        dt_c = dt[rs]
        for g in range(SSD_GROUPS):
            c_g = cm[rs, g * SSD_STATE:(g + 1) * SSD_STATE]
            bt_g = bm_t[g * SSD_STATE:(g + 1) * SSD_STATE, rs]
            cb = _dot(c_g, bt_g)
            for jj in range(2):
                j = 2 * g + jj
                h0, h1 = 2 * j, 2 * j + 1
                col0, col1 = acs[:, h0:h0 + 1], acs[:, h1:h1 + 1]
                row0, row1 = acs_t[h0:h0 + 1, :], acs_t[h1:h1 + 1, :]
                l0 = jnp.exp(jnp.where(causal, col0 - row0, -jnp.inf))
                l1 = jnp.exp(jnp.where(causal, col1 - row1, -jnp.inf))
                mm = jnp.concatenate([(cb * l0).astype(BF16), (cb * l1).astype(BF16)], axis=0)
                xp = xs[rs, j * LANES:(j + 1) * LANES]
                xdt = xp * jnp.where(left, dt_c[:, h0:h0 + 1], dt_c[:, h1:h1 + 1])
                yy = _dot(mm, xdt.astype(BF16))
                y_diag = jnp.where(left, yy[:lc], yy[lc:])
                acs_p = jnp.where(left, col0, col1)
                st = state_ref[j]
                y_off = _dot(c_g, st.astype(BF16)) * jnp.exp(acs_p)
                last_p = jnp.where(left, acs[lc - 1:lc, h0:h0 + 1], acs[lc - 1:lc, h1:h1 + 1])
                xd = (xdt * jnp.exp(last_p - acs_p)).astype(BF16)
                state_ref[j] = st * jnp.exp(last_p) + _dot(bt_g, xd)
                y_ref[rs, j * LANES:(j + 1) * LANES] = (
                    y_diag + y_off + xp * dskip_ref[:, j * LANES:(j + 1) * LANES])

    gy = y_ref[...] * (z * _sigmoid(z))
    gw = SSD_INNER // SSD_GROUPS
    outs = []
    for g in range(SSD_GROUPS):
        gg = gy[:, g * gw:(g + 1) * gw]
        outs.append(gg * lax.rsqrt(jnp.mean(gg * gg, axis=-1, keepdims=True) + RMS_EPS))
    o_ref[...] = (jnp.concatenate(outs, axis=1) * nw_ref[...]).astype(BF16)


def _ssd_call(x3, meta_tile, g, b, wz, wxbc, wdt, cw, cb, dtb, alog, dskip, nw):
    bsz, s, _ = x3.shape
    lt = SSD_TILE
    nt = s // lt
    xspec = pl.BlockSpec((None, lt, D_MODEL), lambda bb, t: (bb, jnp.maximum(t - 1, 0), 0))
    return pl.pallas_call(
        _ssd_kernel,
        grid=(bsz, nt + 1),
        in_specs=[xspec, _const_spec((lt, D_MODEL)), _const_spec((1, D_MODEL)),
                  _const_spec((1, D_MODEL)), _const_spec((D_MODEL, SSD_INNER)),
                  _const_spec((D_MODEL, SSD_CONV_DIM)), _const_spec((D_MODEL, LANES)),
                  _const_spec((SSD_CONV, SSD_CONV_DIM)), _const_spec((1, SSD_CONV_DIM)),
                  _const_spec((1, LANES)), _const_spec((1, LANES)), _const_spec((1, SSD_INNER)),
                  _const_spec((1, SSD_INNER))],
        out_specs=xspec,
        out_shape=jax.ShapeDtypeStruct((bsz, s, SSD_INNER), BF16),
        scratch_shapes=[pltpu.VMEM((SSD_HEADS // 2, SSD_STATE, LANES), F32),
                        pltpu.VMEM((lt + 8, SSD_CONV_DIM), F32),
                        pltpu.VMEM((lt, SSD_INNER), F32)],
        compiler_params=_cparams(("parallel", "arbitrary")),
    )(x3, meta_tile, g, b, wz, wxbc, wdt, cw, cb, dtb, alog, dskip, nw)


def _merge_kernel(x_ref, ssd_ref, da_ref, g0_ref, b0_ref, wg_ref, bg_ref, wso_ref, wdo_ref, wo_ref,
                  g1_ref, b1_ref, wr_ref, br_ref, h1_ref, hb_ref, lp_ref, tab_ref):
    tm = x_ref.shape[0]
    d = D_MODEL

    h = _layer_norm(x_ref[...], g0_ref[...], b0_ref[...])
    gates = _sigmoid(_dot(h.astype(BF16), wg_ref[...]) + bg_ref[...])
    y_ssd = _dot(ssd_ref[...], wso_ref[...])
    y_da = _dot(da_ref[...], wdo_ref[...])
    merged = gates[:, :d] * y_ssd + gates[:, d:] * y_da
    mix = _dot(merged.astype(BF16), wo_ref[...])
    h1 = _layer_norm(DEEPNORM_ALPHA * h + mix, g1_ref[...], b1_ref[...])
    h1_ref[...] = h1
    hb = h1.astype(BF16)
    hb_ref[...] = hb

    lane = lax.broadcasted_iota(jnp.int32, (tm, LANES), 1)
    logits = _dot(hb, wr_ref[...]) + br_ref[...]
    logits = jnp.where(lane < N_EXPERTS, logits, -jnp.inf)
    vals, sels = [], []
    for k in range(TOP_K):
        mx = jnp.max(logits, axis=1, keepdims=True)
        idx = jnp.min(jnp.where(logits == mx, lane, LANES), axis=1, keepdims=True)
        sel = lane == idx
        logits = jnp.where(sel, -jnp.inf, logits)
        vals.append(mx)
        sels.append(sel)
    exps = [jnp.exp(v - vals[0]) for v in vals]
    den = exps[0] + exps[1] + exps[2] + exps[3]

    onehot = jnp.where(sels[0] | sels[1] | sels[2] | sels[3], 1.0, 0.0)
    r_i = lax.broadcasted_iota(jnp.int32, (tm, tm), 0)
    c_i = lax.broadcasted_iota(jnp.int32, (tm, tm), 1)
    strict = jnp.where(c_i < r_i, 1.0, 0.0).astype(BF16)
    before = _dot(strict, onehot.astype(BF16))
    cnt = jnp.sum(onehot, axis=0, keepdims=True)
    cnt_al = jnp.floor((cnt + (SEG_ALIGN - 1)) * (1.0 / SEG_ALIGN)) * SEG_ALIGN
    e_r = lax.broadcasted_iota(jnp.int32, (LANES, LANES), 0)
    e_c = lax.broadcasted_iota(jnp.int32, (LANES, LANES), 1)
    upper = jnp.where(e_r < e_c, 1.0, 0.0).astype(BF16)
    start = _dot(jnp.broadcast_to(cnt_al, (8, LANES)).astype(BF16), upper)[0:1, :]
    pos = before + start
    lp = jnp.zeros((tm, LANES), F32)
    for k in range(TOP_K):
        lp = jnp.where(lane == k, jnp.sum(jnp.where(sels[k], pos, 0.0), axis=1, keepdims=True), lp)
        lp = jnp.where(lane == TOP_K + k, exps[k] / den, lp)
    lp_ref[...] = lp
    row8 = lax.broadcasted_iota(jnp.int32, (8, LANES), 0)
    tab_ref[...] = jnp.where(row8 == 0, cnt, jnp.where(row8 == 1, cnt_al, jnp.where(row8 == 2, start, 0.0)))


def _merge_call(x2d, ssd_n, da_n, g0, b0, wg, bg, wso, wdo, wo, g1, b1, wr, br, tm):
    n = x2d.shape[0]
    d = D_MODEL
    rowf = pl.BlockSpec((tm, d), lambda i: (i, 0))
    rowl = pl.BlockSpec((tm, LANES), lambda i: (i, 0))
    return pl.pallas_call(
        _merge_kernel,
        grid=(n // tm,),
        in_specs=[rowf, rowf, rowf, _const_spec((1, d)), _const_spec((1, d)),
                  _const_spec((d, 2 * d)), _const_spec((1, 2 * d)), _const_spec((d, d)),
                  _const_spec((d, d)), _const_spec((d, d)), _const_spec((1, d)), _const_spec((1, d)),
                  _const_spec((d, LANES)), _const_spec((1, LANES))],
        out_specs=[rowf, rowf, rowl, pl.BlockSpec((8, LANES), lambda i: (i, 0))],
        out_shape=[jax.ShapeDtypeStruct((n, d), F32), jax.ShapeDtypeStruct((n, d), BF16),
                   jax.ShapeDtypeStruct((n, LANES), F32),
                   jax.ShapeDtypeStruct((n // tm * 8, LANES), F32)],
        compiler_params=_cparams(("parallel",)),
    )(x2d, ssd_n, da_n, g0, b0, wg, bg, wso, wdo, wo, g1, b1, wr, br)


def _run_copy(src_ref, dst_ref, s_al, d_al, n_al, sem, wait):
    for b in reversed(range(SEG_BITS)):
        size = (1 << b) * SEG_ALIGN
        done = (n_al >> (b + 1)) << (b + 1)

        @pl.when((n_al & (1 << b)) != 0)
        def _():
            s0 = pl.multiple_of((s_al + done) * SEG_ALIGN, SEG_ALIGN)
            d0 = pl.multiple_of((d_al + done) * SEG_ALIGN, SEG_ALIGN)
            cp = pltpu.make_async_copy(src_ref.at[pl.ds(s0, size), :], dst_ref.at[pl.ds(d0, size), :], sem)
            if wait:
                cp.wait()
            else:
                cp.start()


def _hits(lp_t, k, rows, r0):
    r = lax.broadcasted_iota(jnp.int32, (rows, lp_t.shape[1]), 0) + r0
    return r == lp_t[k:k + 1, :].astype(jnp.int32)


def _sort_kernel(glob_t, loc_t, len_t, fdst_t, flen_t, hb_ref, lp_ref, xs_ref, buf_ref, zero_ref, sems):
    d = D_MODEL
    i = pl.program_id(0)
    last = pl.num_programs(0) - 1

    def sort_into(slot):
        lp_t = lp_ref[...].T
        hb = hb_ref[...]
        for c in range(SORT_ROWS // SORT_CHUNK):
            r0 = c * SORT_CHUNK
            hits = [_hits(lp_t, k, SORT_CHUNK, r0) for k in range(TOP_K)]
            sel = jnp.where(hits[0] | hits[1] | hits[2] | hits[3], 1.0, 0.0).astype(BF16)
            xsort = _dot(sel, hb)
            bits = pltpu.bitcast(xsort, jnp.uint32)
            buf_ref[slot, r0:r0 + SORT_CHUNK, :d // 2] = bits[:, :d // 2] | (bits[:, d // 2:] >> 16)
            g = jnp.where(hits[0], lp_t[TOP_K:TOP_K + 1, :], 0.0)
            for k in range(1, TOP_K):
                g = g + jnp.where(hits[k], lp_t[TOP_K + k:TOP_K + k + 1, :], 0.0)
            gsum = jnp.sum(g, axis=1, keepdims=True)
            buf_ref[slot, r0:r0 + SORT_CHUNK, d // 2:] = pltpu.bitcast(
                jnp.broadcast_to(gsum, (SORT_CHUNK, LANES)), jnp.uint32)

    def runs(tile, slot, wait):
        def one(e, carry):
            s = tile * N_EXPERTS + e
            _run_copy(buf_ref.at[slot], xs_ref, loc_t[s], glob_t[s], len_t[s], sems.at[slot], wait)
            return carry

        lax.fori_loop(0, N_EXPERTS, one, 0)

    def fills(wait):
        def one(k, carry):
            _run_copy(zero_ref, xs_ref, 0, fdst_t[k], flen_t[k], sems.at[2], wait)
            return carry

        lax.fori_loop(0, fdst_t.shape[0], one, 0)

    for slot in range(2):
        @pl.when(i % 2 == slot)
        def _():
            sort_into(slot)
            runs(i, slot, False)

            @pl.when(i > 0)
            def _():
                runs(i - 1, 1 - slot, True)

            @pl.when(i == last)
            def _():
                zero_ref[...] = jnp.zeros_like(zero_ref)
                fills(False)
                runs(i, slot, True)
                fills(True)


def _sort_call(glob_t, loc_t, len_t, fdst_t, flen_t, hb, lp, tm, out_rows):
    n, d = hb.shape
    grid_spec = pltpu.PrefetchScalarGridSpec(
        num_scalar_prefetch=5,
        grid=(n // tm,),
        in_specs=[pl.BlockSpec((tm, d), lambda i, *_: (i, 0)), pl.BlockSpec((tm, LANES), lambda i, *_: (i, 0))],
        out_specs=pl.BlockSpec(memory_space=pl.ANY),
        scratch_shapes=[pltpu.VMEM((2, SORT_ROWS, XS_WIDTH), jnp.uint32),
                        pltpu.VMEM((EXPERT_TILE, XS_WIDTH), jnp.uint32),
                        pltpu.SemaphoreType.DMA((3,))],
    )
    return pl.pallas_call(
        _sort_kernel,
        grid_spec=grid_spec,
        out_shape=jax.ShapeDtypeStruct((out_rows, XS_WIDTH), jnp.uint32),
        compiler_params=_cparams(("arbitrary",)),
    )(glob_t, loc_t, len_t, fdst_t, flen_t, hb, lp)


def _expert_kernel(tile_ref, exp_ref, lo_ref, hi_ref, xs_ref, wgu_ref, bgu_ref, wd_ref, bd_ref, ys_ref,
                   wgu_bf_ref, wd_bf_ref):
    i = pl.program_id(0)
    lo = lo_ref[i]
    hi = hi_ref[i]
    tm = xs_ref.shape[0]
    d = D_MODEL

    @pl.when(jnp.logical_or(i == 0, exp_ref[i] != exp_ref[jnp.maximum(i - 1, 0)]))
    def _():
        wgu_bf_ref[...] = wgu_ref[...].astype(BF16)
        wd_bf_ref[...] = wd_ref[...].astype(BF16)

    @pl.when(lo > hi)
    def _():
        ys_ref[...] = jnp.zeros_like(ys_ref)

    @pl.when(lo < hi)
    def _():
        w = xs_ref[:, :d // 2]
        x_hi = pltpu.bitcast(w & jnp.uint32(0xFFFF0000), F32).astype(BF16)
        x_lo = pltpu.bitcast(w << 16, F32).astype(BF16)
        x = jnp.concatenate([x_hi, x_lo], axis=1)
        hid = _dot(x, wgu_bf_ref[...]) + bgu_ref[...]
        gate = jnp.minimum(hid[:, :D_FF], SWIGLU_LIMIT)
        up = jnp.clip(hid[:, D_FF:], -SWIGLU_LIMIT, SWIGLU_LIMIT)
        act = gate * _sigmoid(SWIGLU_ALPHA * gate) * (up + 1.0)
        y = _dot(act.astype(BF16), wd_bf_ref[...]) + bd_ref[...]
        rgate = pltpu.bitcast(xs_ref[:, d // 2:], F32)
        y = y * jnp.concatenate([rgate] * (d // LANES), axis=1)
        row = lax.broadcasted_iota(jnp.int32, (tm, 1), 0)
        mine = jnp.logical_and(row >= lo, row < hi)

        @pl.when(lo == 0)
        def _():
            ys_ref[...] = jnp.where(mine, y, 0.0)

        @pl.when(lo > 0)
        def _():
            ys_ref[...] = jnp.where(mine, y, ys_ref[...])


def _expert_call(item_tile, item_exp, item_lo, item_hi, xs, wgu, bgu, wd, bd):
    m, w = xs.shape
    tm = EXPERT_TILE
    n_items = item_tile.shape[0]
    grid_spec = pltpu.PrefetchScalarGridSpec(
        num_scalar_prefetch=4,
        grid=(n_items,),
        in_specs=[pl.BlockSpec((tm, w), lambda i, t, e, lo, hi: (t[i], 0)),
                  pl.BlockSpec((None, D_MODEL, 2 * D_FF), lambda i, t, e, lo, hi: (e[i], 0, 0)),
                  pl.BlockSpec((None, 1, 2 * D_FF), lambda i, t, e, lo, hi: (e[i], 0, 0)),
                  pl.BlockSpec((None, D_FF, D_MODEL), lambda i, t, e, lo, hi: (e[i], 0, 0)),
                  pl.BlockSpec((None, 1, D_MODEL), lambda i, t, e, lo, hi: (e[i], 0, 0))],
        out_specs=pl.BlockSpec((tm, D_MODEL), lambda i, t, e, lo, hi: (t[i], 0)),
        scratch_shapes=[pltpu.VMEM((D_MODEL, 2 * D_FF), BF16), pltpu.VMEM((D_FF, D_MODEL), BF16)],
    )
    return pl.pallas_call(
        _expert_kernel,
        grid_spec=grid_spec,
        out_shape=jax.ShapeDtypeStruct((m, D_MODEL), F32),
        compiler_params=_cparams(("arbitrary",)),
    )(item_tile, item_exp, item_lo, item_hi, xs, wgu, bgu, wd, bd)


def _combine_kernel(glob_t, loc_t, len_t, ys_ref, lp_ref, h1_ref, g2_ref, b2_ref, o_ref, ybuf_ref, sems):
    tm = h1_ref.shape[0]
    i = pl.program_id(0)
    n_steps = pl.num_programs(0)

    def fetch(tile, slot, wait):
        if not wait:
            ybuf_ref[slot, TOP_K * tm:, :] = jnp.zeros((SORT_ROWS - TOP_K * tm, D_MODEL), F32)

        def one(e, carry):
            s = tile * N_EXPERTS + e
            _run_copy(ys_ref, ybuf_ref.at[slot], glob_t[s], loc_t[s], len_t[s], sems.at[slot], wait)
            return carry

        lax.fori_loop(0, N_EXPERTS, one, 0)

    def reduce_from(slot):
        lp = lp_ref[...]
        ffn = jnp.zeros((tm, D_MODEL), F32)
        for c in range(SORT_ROWS // SORT_CHUNK):
            r0 = c * SORT_CHUNK
            r = lax.broadcasted_iota(jnp.int32, (tm, SORT_CHUNK), 1) + r0
            hit = r == lp[:, 0:1].astype(jnp.int32)
            for k in range(1, TOP_K):
                hit = hit | (r == lp[:, k:k + 1].astype(jnp.int32))
            sel = jnp.where(hit, 1.0, 0.0).astype(BF16)
            y = ybuf_ref[slot, r0:r0 + SORT_CHUNK, :]
            y_hi = y.astype(BF16)
            y_lo = (y - y_hi.astype(F32)).astype(BF16)
            ffn = ffn + _dot(sel, y_hi) + _dot(sel, y_lo)
        o_ref[...] = _layer_norm(DEEPNORM_ALPHA * h1_ref[...] + ffn, g2_ref[...], b2_ref[...])

    @pl.when(i == 0)
    def _():
        fetch(0, 0, False)

    for slot in range(2):
        @pl.when(i % 2 == slot)
        def _():
            @pl.when(i + 1 < n_steps)
            def _():
                fetch(i + 1, 1 - slot, False)

            fetch(i, slot, True)
            reduce_from(slot)


def _combine_call(glob_t, loc_t, len_t, ys, lp, h1, g2, b2, tm):
    n, d = h1.shape
    rowf = pl.BlockSpec((tm, d), lambda i, *_: (i, 0))
    grid_spec = pltpu.PrefetchScalarGridSpec(
        num_scalar_prefetch=3,
        grid=(n // tm,),
        in_specs=[pl.BlockSpec(memory_space=pl.ANY), pl.BlockSpec((tm, LANES), lambda i, *_: (i, 0)),
                  rowf, pl.BlockSpec((1, d), lambda i, *_: (0, 0)), pl.BlockSpec((1, d), lambda i, *_: (0, 0))],
        out_specs=rowf,
        scratch_shapes=[pltpu.VMEM((2, SORT_ROWS, d), F32), pltpu.SemaphoreType.DMA((2,))],
    )
    return pl.pallas_call(
        _combine_kernel,
        grid_spec=grid_spec,
        out_shape=jax.ShapeDtypeStruct((n, d), F32),
        compiler_params=_cparams(("arbitrary",)),
    )(glob_t, loc_t, len_t, ys, lp, h1, g2, b2)


def _work_items(counts, m):
    tm = EXPERT_TILE
    n_tiles = m // tm
    max_items = n_tiles + N_EXPERTS - 1
    grp_end = jnp.cumsum(counts)
    grp_start = grp_end - counts
    first_tile = grp_start // tm
    last_tile = (grp_end - 1) // tm
    n_e = jnp.where(counts > 0, last_tile - first_tile + 1, 0)
    item_end = jnp.cumsum(n_e)
    item_start = item_end - n_e
    total = item_end[-1]
    i = jnp.arange(max_items, dtype=jnp.int32)
    valid = i < total
    ic = jnp.minimum(i, total - 1)
    e = jnp.minimum(jnp.sum(item_end[None, :] <= ic[:, None], axis=1), N_EXPERTS - 1).astype(jnp.int32)
    tile = first_tile[e] + (ic - item_start[e])
    lo = jnp.maximum(grp_start[e], tile * tm) - tile * tm
    hi = jnp.minimum(grp_end[e], (tile + 1) * tm) - tile * tm
    used_tiles = (grp_end[-1] + tm - 1) // tm
    fill_tile = used_tiles + (i - total)
    is_fill = jnp.logical_and(jnp.logical_not(valid), fill_tile < n_tiles)
    tile = jnp.where(valid, tile, jnp.where(is_fill, fill_tile, n_tiles - 1))
    lo = jnp.where(valid, lo, jnp.where(is_fill, 1, 0))
    hi = jnp.where(valid, hi, 0)
    return tile.astype(jnp.int32), e, lo.astype(jnp.int32), hi.astype(jnp.int32), grp_start


def _segment_tables(tab, n_tiles):
    al = SEG_ALIGN
    t3 = tab.reshape(n_tiles, 8, LANES)
    cnt_al = t3[:, 1, :N_EXPERTS].astype(jnp.int32)
    start = t3[:, 2, :N_EXPERTS].astype(jnp.int32)
    grp = jnp.sum(cnt_al, axis=0)
    grp_start = jnp.cumsum(grp) - grp
    ahead = jnp.cumsum(cnt_al, axis=0) - cnt_al
    local = start.reshape(-1) // al
    glob = (grp_start[None, :] + ahead).reshape(-1) // al
    length = cnt_al.reshape(-1) // al
    return grp, local, glob, length, jnp.sum(grp)


def kernel(x, meta_tokens, ln_in_g, ln_in_b, w_in, b_gate, conv_w, conv_b, dt_bias, a_log, d_skip, ssd_norm_w, w_ssd_out, lam_q1, lam_k1, lam_q2, lam_k2, subln_w, w_da_out, w_out, ln1_g, ln1_b, w_router, b_router, w_gate_up, b_gate_up, w_down, b_down, ln2_g, ln2_b):
    bsz, s, d = x.shape
    n = bsz * s
    l = 0
    row = lambda v: v.reshape(1, -1).astype(F32)

    w = w_in[l]
    c0 = SSD_INNER
    c1 = c0 + SSD_CONV_DIM
    c2 = c1 + SSD_HEADS
    c3 = c2 + 3 * D_MODEL
    w_z = w[:, :c0].astype(BF16)
    w_xbc = w[:, c0:c1].astype(BF16)
    w_dt = jnp.pad(w[:, c1:c2], ((0, 0), (0, LANES - SSD_HEADS))).astype(BF16)
    w_qkv = w[:, c2:c3].astype(BF16)
    w_g = w[:, c3:].astype(BF16)
    g0, b0 = row(ln_in_g), row(ln_in_b)

    x2d = x.reshape(n, d)
    q, k, v = _qkv_call(x2d, g0, b0, w_qkv, ROW_TILE)
    _, km, vm = _qkv_call(meta_tokens.astype(F32), g0, b0, w_qkv, N_META)
    km = jnp.pad(km, ((0, LANES - N_META), (0, 0)))
    vm = jnp.pad(vm, ((0, LANES - N_META), (0, 0)))

    lam = (jnp.exp(jnp.sum(lam_q1[l].astype(F32) * lam_k1[l].astype(F32)))
           - jnp.exp(jnp.sum(lam_q2[l].astype(F32) * lam_k2[l].astype(F32))) + LAMBDA_INIT)
    da_n = _attn_call(lam.reshape(1, 1), q.reshape(bsz, s, d), k.reshape(bsz, s, d),
                      v.reshape(bsz, s, d), km, vm, row(subln_w[l]))

    meta_tile = jnp.pad(meta_tokens.astype(F32), ((SSD_TILE - N_META, 0), (0, 0)))
    pad_h = lambda vec: jnp.pad(row(vec), ((0, 0), (0, LANES - SSD_HEADS)))
    ssd_n = _ssd_call(x, meta_tile, g0, b0, w_z, w_xbc, w_dt, conv_w[l].astype(F32), row(conv_b[l]),
                      pad_h(dt_bias[l]), pad_h(a_log[l]),
                      row(jnp.repeat(d_skip[l].astype(F32), SSD_HEAD_DIM)), row(ssd_norm_w[l]))

    w_r = jnp.pad(w_router[l], ((0, 0), (0, LANES - N_EXPERTS))).astype(BF16)
    b_r = jnp.pad(row(b_router[l]), ((0, 0), (0, LANES - N_EXPERTS)))
    h1, hb, lp, tab = _merge_call(
        x2d, ssd_n.reshape(n, d), da_n.reshape(n, d), g0, b0, w_g, row(b_gate[l]),
        w_ssd_out[l].astype(BF16), w_da_out[l].astype(BF16), w_out[l].astype(BF16),
        row(ln1_g[l]), row(ln1_b[l]), w_r, b_r, ROW_TILE)
    n_tiles = n // ROW_TILE
    m_rows = n * TOP_K + n_tiles * N_EXPERTS * SEG_ALIGN
    m_rows = -(-m_rows // EXPERT_TILE) * EXPERT_TILE
    grp, local, glob, length, total = _segment_tables(tab, n_tiles)
    item_tile, item_exp, item_lo, item_hi, _ = _work_items(grp, m_rows)

    tail_len = (-total) % EXPERT_TILE
    fill_dst = total + tail_len + jnp.arange(m_rows // EXPERT_TILE - n * TOP_K // EXPERT_TILE + 1,
                                             dtype=jnp.int32) * EXPERT_TILE
    fill_len = jnp.where(fill_dst + EXPERT_TILE <= m_rows, EXPERT_TILE, 0)
    fdst = jnp.concatenate([total[None], fill_dst]).astype(jnp.int32) // SEG_ALIGN
    flen = jnp.concatenate([tail_len[None], fill_len]).astype(jnp.int32) // SEG_ALIGN
    xs = _sort_call(glob, local, length, fdst, flen, hb, lp, ROW_TILE, m_rows)

    ys = _expert_call(item_tile, item_exp, item_lo, item_hi, xs,
                      w_gate_up[l].astype(F32), b_gate_up[l].reshape(N_EXPERTS, 1, -1).astype(F32),
                      w_down[l].astype(F32), b_down[l].reshape(N_EXPERTS, 1, -1).astype(F32))

    out = _combine_call(glob, local, length, ys, lp, h1, row(ln2_g[l]), row(ln2_b[l]), ROW_TILE)
    return out.reshape(bsz, s, d)
```

```python
import functools
import math

import jax
import jax.numpy as jnp
from jax import lax
from jax.experimental import pallas as pl
from jax.experimental.pallas import tpu as pltpu

F32 = jnp.float32
BF16 = jnp.bfloat16

D_MODEL = 1024
N_META = 16
SSD_HEADS = 16
SSD_HEAD_DIM = 64
SSD_INNER = 1024
SSD_GROUPS = 4
SSD_STATE = 128
SSD_CONV = 4
SSD_CONV_DIM = 2048
DA_HEADS = 8
DA_HEAD_DIM = 64
N_EXPERTS = 32
TOP_K = 4
D_FF = 1024
SWIGLU_LIMIT = 7.0
SWIGLU_ALPHA = 1.702
DEPTH = 1
DEEPNORM_ALPHA = (2.0 * DEPTH) ** 0.25
LN_EPS = 1e-5
RMS_EPS = 1e-6
LAMBDA_INIT = 0.8 - 0.6 * math.exp(-0.3 * 0)
LOG2_E = math.log2(math.e)

LANES = 128
VMEM_LIMIT = 56 * 1024 * 1024

ROW_TILE = 512
ATT_TQ = 512
ATT_TK = 512
ATT_HEADS = 4
SSD_TILE = 256
SSD_CHUNK = 128
EXPERT_TILE = 512
SEG_ALIGN = 8
SEG_BITS = 7
SORT_ROWS = ROW_TILE * 4 + 32 * SEG_ALIGN
SORT_CHUNK = SORT_ROWS // 3
XS_WIDTH = 512 + 128


def _cparams(sem):
    return pltpu.CompilerParams(dimension_semantics=sem, vmem_limit_bytes=VMEM_LIMIT)


def _const_spec(shape):
    nd = len(shape)
    return pl.BlockSpec(shape, lambda *a: (0,) * nd)


def _layer_norm(x, g, b):
    mu = jnp.mean(x, axis=-1, keepdims=True)
    xc = x - mu
    var = jnp.mean(xc * xc, axis=-1, keepdims=True)
    return xc * lax.rsqrt(var + LN_EPS) * g + b


def _sigmoid(x):
    return 1.0 / (1.0 + jnp.exp(-x))


def _dot(a, b):
    return jnp.dot(a, b, preferred_element_type=F32)


def _dot_nt(a, b):
    return lax.dot_general(a, b, (((1,), (1,)), ((), ())), preferred_element_type=F32)


def _qkv_kernel(x_ref, g_ref, b_ref, w_ref, q_ref, k_ref, v_ref):
    h = _layer_norm(x_ref[...], g_ref[...], b_ref[...]).astype(BF16)
    acc = _dot(h, w_ref[...])
    d = D_MODEL
    q_ref[...] = (acc[:, :d] * (DA_HEAD_DIM ** -0.5 * LOG2_E)).astype(BF16)
    k_ref[...] = acc[:, d:2 * d].astype(BF16)
    v_ref[...] = acc[:, 2 * d:].astype(BF16)


def _qkv_call(x2d, g, b, w_qkv, tm):
    n = x2d.shape[0]
    out = jax.ShapeDtypeStruct((n, D_MODEL), BF16)
    row = pl.BlockSpec((tm, D_MODEL), lambda i: (i, 0))
    return pl.pallas_call(
        _qkv_kernel,
        grid=(n // tm,),
        in_specs=[row, _const_spec((1, D_MODEL)), _const_spec((1, D_MODEL)),
                  _const_spec((D_MODEL, 3 * D_MODEL))],
        out_specs=[row, row, row],
        out_shape=[out, out, out],
        compiler_params=_cparams(("parallel",)),
    )(x2d, g, b, w_qkv)


def _attn_kernel(lam_ref, q_ref, k_ref, v_ref, km_ref, vm_ref, sw_ref, o_ref,
                 sa_ref, sb_ref, m_ref, l_ref, acc_ref):
    tq, tk = ATT_TQ, ATT_TK
    i = pl.program_id(2)
    lane = lax.broadcasted_iota(jnp.int32, (1, LANES), 1)
    heads = range(ATT_HEADS)
    hs = [slice(h * LANES, (h + 1) * LANES) for h in heads]

    def stacked_q(h):
        q = q_ref[:, hs[h]]
        zero = jnp.zeros_like(q)
        return jnp.concatenate([jnp.where(lane < DA_HEAD_DIM, q, zero),
                                jnp.where(lane >= DA_HEAD_DIM, q, zero)], axis=0)

    q2 = [stacked_q(h) for h in heads]

    def scores(h, j):
        off = pl.multiple_of(j * tk, tk)
        return _dot_nt(q2[h], k_ref[pl.ds(off, tk), hs[h]])

    def absorb(h, s, j):
        m_old = m_ref[h]
        m_new = jnp.maximum(m_old, jnp.max(s, axis=1, keepdims=True))
        alpha = jnp.exp2(m_old - m_new)
        p = jnp.exp2(s - jnp.concatenate([m_new] * (tk // LANES), axis=1))
        psum = p[:, :LANES]
        for c in range(1, tk // LANES):
            psum = psum + p[:, c * LANES:(c + 1) * LANES]
        l_ref[h] = alpha * l_ref[h] + psum
        off = pl.multiple_of(j * tk, tk)
        acc_ref[h] = alpha * acc_ref[h] + _dot(p.astype(BF16), v_ref[pl.ds(off, tk), hs[h]])
        m_ref[h] = m_new

    meta_ok = lax.broadcasted_iota(jnp.int32, (1, LANES), 1) < N_META
    for h in heads:
        sm = jnp.where(meta_ok, _dot_nt(q2[h], km_ref[:, hs[h]]), -jnp.inf)
        m0 = jnp.broadcast_to(jnp.max(sm, axis=1, keepdims=True), sm.shape)
        p0 = jnp.exp2(sm - m0)
        m_ref[h] = m0
        l_ref[h] = p0
        acc_ref[h] = _dot(p0.astype(BF16), vm_ref[:, hs[h]])

    for h in heads:
        sa_ref[h] = scores(h, 0)

    def body(j, carry):
        @pl.when(j % 2 == 0)
        def _():
            nxt = [scores(h, j + 1) for h in heads]
            for h in heads:
                absorb(h, sa_ref[h], j)
            for h in heads:
                sb_ref[h] = nxt[h]

        @pl.when(j % 2 == 1)
        def _():
            nxt = [scores(h, j + 1) for h in heads]
            for h in heads:
                absorb(h, sb_ref[h], j)
            for h in heads:
                sa_ref[h] = nxt[h]

        return carry

    lax.fori_loop(0, i, body, 0)

    qc = (lax.broadcasted_iota(jnp.int32, (2 * tq, tk), 0) % tq) // 64
    kc = lax.broadcasted_iota(jnp.int32, (2 * tq, tk), 1) // 64
    vis = kc <= qc

    @pl.when(i % 2 == 0)
    def _():
        for h in heads:
            absorb(h, jnp.where(vis, sa_ref[h], -jnp.inf), i)

    @pl.when(i % 2 == 1)
    def _():
        for h in heads:
            absorb(h, jnp.where(vis, sb_ref[h], -jnp.inf), i)

    for h in heads:
        a = acc_ref[h] / jnp.sum(l_ref[h], axis=1, keepdims=True)
        o = a[:tq] - lam_ref[0, 0] * a[tq:]
        o = o * lax.rsqrt(jnp.mean(o * o, axis=-1, keepdims=True) + RMS_EPS) * sw_ref[...]
        o_ref[:, hs[h]] = (o * (1.0 - LAMBDA_INIT)).astype(BF16)


def _attn_call(lam, q, k, v, km, vm, subln_w):
    bsz, s, _ = q.shape
    nq = s // ATT_TQ
    hw = ATT_HEADS * LANES
    qspec = pl.BlockSpec((None, ATT_TQ, hw), lambda b, h, i: (b, i, h))
    kvspec = pl.BlockSpec((None, s, hw), lambda b, h, i: (b, 0, h))
    mspec = pl.BlockSpec((LANES, hw), lambda b, h, i: (0, h))
    rows = 2 * ATT_TQ
    return pl.pallas_call(
        _attn_kernel,
        grid=(bsz, DA_HEADS // ATT_HEADS, nq),
        in_specs=[pl.BlockSpec(memory_space=pltpu.SMEM), qspec, kvspec, kvspec, mspec, mspec,
                  _const_spec((1, LANES))],
        out_specs=qspec,
        out_shape=jax.ShapeDtypeStruct((bsz, s, D_MODEL), BF16),
        scratch_shapes=[pltpu.VMEM((ATT_HEADS, rows, ATT_TK), F32), pltpu.VMEM((ATT_HEADS, rows, ATT_TK), F32),
                        pltpu.VMEM((ATT_HEADS, rows, LANES), F32), pltpu.VMEM((ATT_HEADS, rows, LANES), F32),
                        pltpu.VMEM((ATT_HEADS, rows, LANES), F32)],
        compiler_params=_cparams(("parallel", "parallel", "arbitrary")),
    )(lam, q, k, v, km, vm, subln_w)


def _split3(a):
    hi = a.astype(BF16)
    r = a - hi.astype(F32)
    mid = r.astype(BF16)
    lo = (r - mid.astype(F32)).astype(BF16)
    return hi, mid, lo


def _ssd_kernel(x_ref, meta_ref, g_ref, b_ref, wz_ref, wxbc_ref, wdt_ref, cw_ref, cb_ref,
                dtb_ref, alog_ref, dskip_ref, nw_ref, o_ref, state_ref, cbuf_ref, y_ref):
    lt, lc = SSD_TILE, SSD_CHUNK
    t = pl.program_id(1)
    is_meta = t == 0

    @pl.when(is_meta)
    def _():
        state_ref[...] = jnp.zeros_like(state_ref)
        cbuf_ref[0:8, :] = jnp.zeros((8, SSD_CONV_DIM), F32)

    row = lax.broadcasted_iota(jnp.int32, (lt, 1), 0)
    valid = jnp.logical_or(jnp.logical_not(is_meta), row >= lt - N_META)
    x = jnp.where(is_meta, meta_ref[...], x_ref[...])
    h = _layer_norm(x, g_ref[...], b_ref[...]).astype(BF16)
    z = _dot(h, wz_ref[...])
    xbc = jnp.where(valid, _dot(h, wxbc_ref[...]), 0.0)
    dtr = _dot(h, wdt_ref[...])

    cbuf_ref[8:8 + lt, :] = xbc
    conv = (cw_ref[0:1, :] * cbuf_ref[5:5 + lt, :] + cw_ref[1:2, :] * cbuf_ref[6:6 + lt, :]
            + cw_ref[2:3, :] * cbuf_ref[7:7 + lt, :] + cw_ref[3:4, :] * xbc + cb_ref[...])
    cbuf_ref[0:8, :] = cbuf_ref[lt:lt + 8, :]
    act = jnp.where(valid, conv * _sigmoid(conv), 0.0)
    xs = act[:, :SSD_INNER]
    bm = act[:, SSD_INNER:SSD_INNER + SSD_GROUPS * SSD_STATE]
    cm = act[:, SSD_INNER + SSD_GROUPS * SSD_STATE:].astype(BF16)
    bm_t = bm.T.astype(BF16)

    dtv = dtr + dtb_ref[...]
    dt = jnp.maximum(dtv, 0.0) + jnp.log1p(jnp.exp(-jnp.abs(dtv)))
    dt = jnp.where(valid, dt, 0.0)
    a = dt * (-jnp.exp(alog_ref[...]))

    lane = lax.broadcasted_iota(jnp.int32, (1, LANES), 1)
    left = lane < SSD_HEAD_DIM
    tri_r = lax.broadcasted_iota(jnp.int32, (lc, lc), 0)
    tri_c = lax.broadcasted_iota(jnp.int32, (lc, lc), 1)
    causal = tri_c <= tri_r
    tri = jnp.where(causal, 1.0, 0.0).astype(BF16)

    for c in range(lt // lc):
        rs = slice(c * lc, (c + 1) * lc)
        hi, mid, lo = _split3(a[rs])
        acs = _dot(tri, hi) + _dot(tri, mid) + _dot(tri, lo)
        acs_t = acs.T
        dt_c = dt[rs]
        for g in range(SSD_GROUPS):
            c_g = cm[rs, g * SSD_STATE:(g + 1) * SSD_STATE]
            bt_g = bm_t[g * SSD_STATE:(g + 1) * SSD_STATE, rs]
            cb = _dot(c_g, bt_g)
            for jj in range(2):
                j = 2 * g + jj
                h0, h1 = 2 * j, 2 * j + 1
                col0, col1 = acs[:, h0:h0 + 1], acs[:, h1:h1 + 1]
                row0, row1 = acs_t[h0:h0 + 1, :], acs_t[h1:h1 + 1, :]
                l0 = jnp.exp(jnp.where(causal, col0 - row0, -jnp.inf))
                l1 = jnp.exp(jnp.where(causal, col1 - row1, -jnp.inf))
                mm = jnp.concatenate([(cb * l0).astype(BF16), (cb * l1).astype(BF16)], axis=0)
                xp = xs[rs, j * LANES:(j + 1) * LANES]
                xdt = xp * jnp.where(left, dt_c[:, h0:h0 + 1], dt_c[:, h1:h1 + 1])
                yy = _dot(mm, xdt.astype(BF16))
                y_diag = jnp.where(left, yy[:lc], yy[lc:])
                acs_p = jnp.where(left, col0, col1)
                st = state_ref[j]
                y_off = _dot(c_g, st.astype(BF16)) * jnp.exp(acs_p)
                last_p = jnp.where(left, acs[lc - 1:lc, h0:h0 + 1], acs[lc - 1:lc, h1:h1 + 1])
                xd = (xdt * jnp.exp(last_p - acs_p)).astype(BF16)
                state_ref[j] = st * jnp.exp(last_p) + _dot(bt_g, xd)
                y_ref[rs, j * LANES:(j + 1) * LANES] = (
                    y_diag + y_off + xp * dskip_ref[:, j * LANES:(j + 1) * LANES])

    gy = y_ref[...] * (z * _sigmoid(z))
    gw = SSD_INNER // SSD_GROUPS
    outs = []
    for g in range(SSD_GROUPS):
        gg = gy[:, g * gw:(g + 1) * gw]
        outs.append(gg * lax.rsqrt(jnp.mean(gg * gg, axis=-1, keepdims=True) + RMS_EPS))
    o_ref[...] = (jnp.concatenate(outs, axis=1) * nw_ref[...]).astype(BF16)


def _ssd_call(x3, meta_tile, g, b, wz, wxbc, wdt, cw, cb, dtb, alog, dskip, nw):
    bsz, s, _ = x3.shape
    lt = SSD_TILE
    nt = s // lt
    xspec = pl.BlockSpec((None, lt, D_MODEL), lambda bb, t: (bb, jnp.maximum(t - 1, 0), 0))
    return pl.pallas_call(
        _ssd_kernel,
        grid=(bsz, nt + 1),
        in_specs=[xspec, _const_spec((lt, D_MODEL)), _const_spec((1, D_MODEL)),
                  _const_spec((1, D_MODEL)), _const_spec((D_MODEL, SSD_INNER)),
                  _const_spec((D_MODEL, SSD_CONV_DIM)), _const_spec((D_MODEL, LANES)),
                  _const_spec((SSD_CONV, SSD_CONV_DIM)), _const_spec((1, SSD_CONV_DIM)),
                  _const_spec((1, LANES)), _const_spec((1, LANES)), _const_spec((1, SSD_INNER)),
                  _const_spec((1, SSD_INNER))],
        out_specs=xspec,
        out_shape=jax.ShapeDtypeStruct((bsz, s, SSD_INNER), BF16),
        scratch_shapes=[pltpu.VMEM((SSD_HEADS // 2, SSD_STATE, LANES), F32),
                        pltpu.VMEM((lt + 8, SSD_CONV_DIM), F32),
                        pltpu.VMEM((lt, SSD_INNER), F32)],
        compiler_params=_cparams(("parallel", "arbitrary")),
    )(x3, meta_tile, g, b, wz, wxbc, wdt, cw, cb, dtb, alog, dskip, nw)


def _merge_kernel(x_ref, ssd_ref, da_ref, g0_ref, b0_ref, wg_ref, bg_ref, wso_ref, wdo_ref, wo_ref,
                  g1_ref, b1_ref, wr_ref, br_ref, h1_ref, hb_ref, lp_ref, tab_ref):
    tm = x_ref.shape[0]
    d = D_MODEL

    h = _layer_norm(x_ref[...], g0_ref[...], b0_ref[...])
    gates = _sigmoid(_dot(h.astype(BF16), wg_ref[...]) + bg_ref[...])
    y_ssd = _dot(ssd_ref[...], wso_ref[...])
    y_da = _dot(da_ref[...], wdo_ref[...])
    merged = gates[:, :d] * y_ssd + gates[:, d:] * y_da
    mix = _dot(merged.astype(BF16), wo_ref[...])
    h1 = _layer_norm(DEEPNORM_ALPHA * h + mix, g1_ref[...], b1_ref[...])
    h1_ref[...] = h1
    hb = h1.astype(BF16)
    hb_ref[...] = hb

    lane = lax.broadcasted_iota(jnp.int32, (tm, LANES), 1)
    logits = _dot(hb, wr_ref[...]) + br_ref[...]
    logits = jnp.where(lane < N_EXPERTS, logits, -jnp.inf)
    vals, sels = [], []
    for k in range(TOP_K):
        mx = jnp.max(logits, axis=1, keepdims=True)
        idx = jnp.min(jnp.where(logits == mx, lane, LANES), axis=1, keepdims=True)
        sel = lane == idx
        logits = jnp.where(sel, -jnp.inf, logits)
        vals.append(mx)
        sels.append(sel)
    exps = [jnp.exp(v - vals[0]) for v in vals]
    den = exps[0] + exps[1] + exps[2] + exps[3]

    onehot = jnp.where(sels[0] | sels[1] | sels[2] | sels[3], 1.0, 0.0)
    r_i = lax.broadcasted_iota(jnp.int32, (tm, tm), 0)
    c_i = lax.broadcasted_iota(jnp.int32, (tm, tm), 1)
    strict = jnp.where(c_i < r_i, 1.0, 0.0).astype(BF16)
    before = _dot(strict, onehot.astype(BF16))
    cnt = jnp.sum(onehot, axis=0, keepdims=True)
    cnt_al = jnp.floor((cnt + (SEG_ALIGN - 1)) * (1.0 / SEG_ALIGN)) * SEG_ALIGN
    e_r = lax.broadcasted_iota(jnp.int32, (LANES, LANES), 0)
    e_c = lax.broadcasted_iota(jnp.int32, (LANES, LANES), 1)
    upper = jnp.where(e_r < e_c, 1.0, 0.0).astype(BF16)
    start = _dot(jnp.broadcast_to(cnt_al, (8, LANES)).astype(BF16), upper)[0:1, :]
    pos = before + start
    lp = jnp.zeros((tm, LANES), F32)
    for k in range(TOP_K):
        lp = jnp.where(lane == k, jnp.sum(jnp.where(sels[k], pos, 0.0), axis=1, keepdims=True), lp)
        lp = jnp.where(lane == TOP_K + k, exps[k] / den, lp)
    lp_ref[...] = lp
    row8 = lax.broadcasted_iota(jnp.int32, (8, LANES), 0)
    tab_ref[...] = jnp.where(row8 == 0, cnt, jnp.where(row8 == 1, cnt_al, jnp.where(row8 == 2, start, 0.0)))


def _merge_call(x2d, ssd_n, da_n, g0, b0, wg, bg, wso, wdo, wo, g1, b1, wr, br, tm):
    n = x2d.shape[0]
    d = D_MODEL
    rowf = pl.BlockSpec((tm, d), lambda i: (i, 0))
    rowl = pl.BlockSpec((tm, LANES), lambda i: (i, 0))
    return pl.pallas_call(
        _merge_kernel,
        grid=(n // tm,),
        in_specs=[rowf, rowf, rowf, _const_spec((1, d)), _const_spec((1, d)),
                  _const_spec((d, 2 * d)), _const_spec((1, 2 * d)), _const_spec((d, d)),
                  _const_spec((d, d)), _const_spec((d, d)), _const_spec((1, d)), _const_spec((1, d)),
                  _const_spec((d, LANES)), _const_spec((1, LANES))],
        out_specs=[rowf, rowf, rowl, pl.BlockSpec((8, LANES), lambda i: (i, 0))],
        out_shape=[jax.ShapeDtypeStruct((n, d), F32), jax.ShapeDtypeStruct((n, d), BF16),
                   jax.ShapeDtypeStruct((n, LANES), F32),
                   jax.ShapeDtypeStruct((n // tm * 8, LANES), F32)],
        compiler_params=_cparams(("parallel",)),
    )(x2d, ssd_n, da_n, g0, b0, wg, bg, wso, wdo, wo, g1, b1, wr, br)


def _run_copy(src_ref, dst_ref, s_al, d_al, n_al, sem, wait):
    for b in reversed(range(SEG_BITS)):
        size = (1 << b) * SEG_ALIGN
        done = (n_al >> (b + 1)) << (b + 1)

        @pl.when((n_al & (1 << b)) != 0)
        def _():
            s0 = pl.multiple_of((s_al + done) * SEG_ALIGN, SEG_ALIGN)
            d0 = pl.multiple_of((d_al + done) * SEG_ALIGN, SEG_ALIGN)
            cp = pltpu.make_async_copy(src_ref.at[pl.ds(s0, size), :], dst_ref.at[pl.ds(d0, size), :], sem)
            if wait:
                cp.wait()
            else:
                cp.start()


def _hits(lp_t, k, rows, r0):
    r = lax.broadcasted_iota(jnp.int32, (rows, lp_t.shape[1]), 0) + r0
    return r == lp_t[k:k + 1, :].astype(jnp.int32)


def _sort_kernel(glob_t, loc_t, len_t, fdst_t, flen_t, hb_ref, lp_ref, xs_ref, buf_ref, zero_ref, sems):
    d = D_MODEL
    i = pl.program_id(0)
    last = pl.num_programs(0) - 1

    def sort_into(slot):
        lp_t = lp_ref[...].T
        hb = hb_ref[...]
        for c in range(SORT_ROWS // SORT_CHUNK):
            r0 = c * SORT_CHUNK
            hits = [_hits(lp_t, k, SORT_CHUNK, r0) for k in range(TOP_K)]
            sel = jnp.where(hits[0] | hits[1] | hits[2] | hits[3], 1.0, 0.0).astype(BF16)
            xsort = _dot(sel, hb)
            bits = pltpu.bitcast(xsort, jnp.uint32)
            buf_ref[slot, r0:r0 + SORT_CHUNK, :d // 2] = bits[:, :d // 2] | (bits[:, d // 2:] >> 16)
            g = jnp.where(hits[0], lp_t[TOP_K:TOP_K + 1, :], 0.0)
            for k in range(1, TOP_K):
                g = g + jnp.where(hits[k], lp_t[TOP_K + k:TOP_K + k + 1, :], 0.0)
            gsum = jnp.sum(g, axis=1, keepdims=True)
            buf_ref[slot, r0:r0 + SORT_CHUNK, d // 2:] = pltpu.bitcast(
                jnp.broadcast_to(gsum, (SORT_CHUNK, LANES)), jnp.uint32)

    def runs(tile, slot, wait):
        def one(e, carry):
            s = tile * N_EXPERTS + e
            _run_copy(buf_ref.at[slot], xs_ref, loc_t[s], glob_t[s], len_t[s], sems.at[slot], wait)
            return carry

        lax.fori_loop(0, N_EXPERTS, one, 0)

    def fills(wait):
        def one(k, carry):
            _run_copy(zero_ref, xs_ref, 0, fdst_t[k], flen_t[k], sems.at[2], wait)
            return carry

        lax.fori_loop(0, fdst_t.shape[0], one, 0)

    for slot in range(2):
        @pl.when(i % 2 == slot)
        def _():
            sort_into(slot)
            runs(i, slot, False)

            @pl.when(i > 0)
            def _():
                runs(i - 1, 1 - slot, True)

            @pl.when(i == last)
            def _():
                zero_ref[...] = jnp.zeros_like(zero_ref)
                fills(False)
                runs(i, slot, True)
                fills(True)


def _sort_call(glob_t, loc_t, len_t, fdst_t, flen_t, hb, lp, tm, out_rows):
    n, d = hb.shape
    grid_spec = pltpu.PrefetchScalarGridSpec(
        num_scalar_prefetch=5,
        grid=(n // tm,),
        in_specs=[pl.BlockSpec((tm, d), lambda i, *_: (i, 0)), pl.BlockSpec((tm, LANES), lambda i, *_: (i, 0))],
        out_specs=pl.BlockSpec(memory_space=pl.ANY),
        scratch_shapes=[pltpu.VMEM((2, SORT_ROWS, XS_WIDTH), jnp.uint32),
                        pltpu.VMEM((EXPERT_TILE, XS_WIDTH), jnp.uint32),
                        pltpu.SemaphoreType.DMA((3,))],
    )
    return pl.pallas_call(
        _sort_kernel,
        grid_spec=grid_spec,
        out_shape=jax.ShapeDtypeStruct((out_rows, XS_WIDTH), jnp.uint32),
        compiler_params=_cparams(("arbitrary",)),
    )(glob_t, loc_t, len_t, fdst_t, flen_t, hb, lp)


def _expert_kernel(tile_ref, exp_ref, lo_ref, hi_ref, xs_ref, wgu_ref, bgu_ref, wd_ref, bd_ref, ys_ref,
                   wgu_bf_ref, wd_bf_ref):
    i = pl.program_id(0)
    lo = lo_ref[i]
    hi = hi_ref[i]
    tm = xs_ref.shape[0]
    d = D_MODEL

    @pl.when(jnp.logical_or(i == 0, exp_ref[i] != exp_ref[jnp.maximum(i - 1, 0)]))
    def _():
        wgu_bf_ref[...] = wgu_ref[...].astype(BF16)
        wd_bf_ref[...] = wd_ref[...].astype(BF16)

    @pl.when(lo > hi)
    def _():
        ys_ref[...] = jnp.zeros_like(ys_ref)

    @pl.when(lo < hi)
    def _():
        w = xs_ref[:, :d // 2]
        x_hi = pltpu.bitcast(w & jnp.uint32(0xFFFF0000), F32).astype(BF16)
        x_lo = pltpu.bitcast(w << 16, F32).astype(BF16)
        x = jnp.concatenate([x_hi, x_lo], axis=1)
        hid = _dot(x, wgu_bf_ref[...]) + bgu_ref[...]
        gate = jnp.minimum(hid[:, :D_FF], SWIGLU_LIMIT)
        up = jnp.clip(hid[:, D_FF:], -SWIGLU_LIMIT, SWIGLU_LIMIT)
        act = gate * _sigmoid(SWIGLU_ALPHA * gate) * (up + 1.0)
        y = _dot(act.astype(BF16), wd_bf_ref[...]) + bd_ref[...]
        rgate = pltpu.bitcast(xs_ref[:, d // 2:], F32)
        y = y * jnp.concatenate([rgate] * (d // LANES), axis=1)
        row = lax.broadcasted_iota(jnp.int32, (tm, 1), 0)
        mine = jnp.logical_and(row >= lo, row < hi)

        @pl.when(lo == 0)
        def _():
            ys_ref[...] = jnp.where(mine, y, 0.0)

        @pl.when(lo > 0)
        def _():
            ys_ref[...] = jnp.where(mine, y, ys_ref[...])


def _expert_call(item_tile, item_exp, item_lo, item_hi, xs, wgu, bgu, wd, bd):
    m, w = xs.shape
    tm = EXPERT_TILE
    n_items = item_tile.shape[0]
    grid_spec = pltpu.PrefetchScalarGridSpec(
        num_scalar_prefetch=4,
        grid=(n_items,),
        in_specs=[pl.BlockSpec((tm, w), lambda i, t, e, lo, hi: (t[i], 0)),
                  pl.BlockSpec((None, D_MODEL, 2 * D_FF), lambda i, t, e, lo, hi: (e[i], 0, 0)),
                  pl.BlockSpec((None, 1, 2 * D_FF), lambda i, t, e, lo, hi: (e[i], 0, 0)),
                  pl.BlockSpec((None, D_FF, D_MODEL), lambda i, t, e, lo, hi: (e[i], 0, 0)),
                  pl.BlockSpec((None, 1, D_MODEL), lambda i, t, e, lo, hi: (e[i], 0, 0))],
        out_specs=pl.BlockSpec((tm, D_MODEL), lambda i, t, e, lo, hi: (t[i], 0)),
        scratch_shapes=[pltpu.VMEM((D_MODEL, 2 * D_FF), BF16), pltpu.VMEM((D_FF, D_MODEL), BF16)],
    )
    return pl.pallas_call(
        _expert_kernel,
        grid_spec=grid_spec,
        out_shape=jax.ShapeDtypeStruct((m, D_MODEL), F32),
        compiler_params=_cparams(("arbitrary",)),
    )(item_tile, item_exp, item_lo, item_hi, xs, wgu, bgu, wd, bd)


def _combine_kernel(glob_t, loc_t, len_t, ys_ref, lp_ref, h1_ref, g2_ref, b2_ref, o_ref, ybuf_ref, sems):
    tm = h1_ref.shape[0]
    i = pl.program_id(0)
    n_steps = pl.num_programs(0)

    def fetch(tile, slot, wait):
        if not wait:
            ybuf_ref[slot, TOP_K * tm:, :] = jnp.zeros((SORT_ROWS - TOP_K * tm, D_MODEL), F32)

        def one(e, carry):
            s = tile * N_EXPERTS + e
            _run_copy(ys_ref, ybuf_ref.at[slot], glob_t[s], loc_t[s], len_t[s], sems.at[slot], wait)
            return carry

        lax.fori_loop(0, N_EXPERTS, one, 0)

    def reduce_from(slot):
        lp = lp_ref[...]
        ffn = jnp.zeros((tm, D_MODEL), F32)
        for c in range(SORT_ROWS // SORT_CHUNK):
            r0 = c * SORT_CHUNK
            r = lax.broadcasted_iota(jnp.int32, (tm, SORT_CHUNK), 1) + r0
            hit = r == lp[:, 0:1].astype(jnp.int32)
            for k in range(1, TOP_K):
                hit = hit | (r == lp[:, k:k + 1].astype(jnp.int32))
            sel = jnp.where(hit, 1.0, 0.0).astype(BF16)
            y = ybuf_ref[slot, r0:r0 + SORT_CHUNK, :]
            y_hi = y.astype(BF16)
            y_lo = (y - y_hi.astype(F32)).astype(BF16)
            ffn = ffn + _dot(sel, y_hi) + _dot(sel, y_lo)
        o_ref[...] = _layer_norm(DEEPNORM_ALPHA * h1_ref[...] + ffn, g2_ref[...], b2_ref[...])

    @pl.when(i == 0)
    def _():
        fetch(0, 0, False)

    for slot in range(2):
        @pl.when(i % 2 == slot)
        def _():
            @pl.when(i + 1 < n_steps)
            def _():
                fetch(i + 1, 1 - slot, False)

            fetch(i, slot, True)
            reduce_from(slot)


def _combine_call(glob_t, loc_t, len_t, ys, lp, h1, g2, b2, tm):
    n, d = h1.shape
    rowf = pl.BlockSpec((tm, d), lambda i, *_: (i, 0))
    grid_spec = pltpu.PrefetchScalarGridSpec(
        num_scalar_prefetch=3,
        grid=(n // tm,),
        in_specs=[pl.BlockSpec(memory_space=pl.ANY), pl.BlockSpec((tm, LANES), lambda i, *_: (i, 0)),
                  rowf, pl.BlockSpec((1, d), lambda i, *_: (0, 0)), pl.BlockSpec((1, d), lambda i, *_: (0, 0))],
        out_specs=rowf,
        scratch_shapes=[pltpu.VMEM((2, SORT_ROWS, d), F32), pltpu.SemaphoreType.DMA((2,))],
    )
    return pl.pallas_call(
        _combine_kernel,
        grid_spec=grid_spec,
        out_shape=jax.ShapeDtypeStruct((n, d), F32),
        compiler_params=_cparams(("arbitrary",)),
    )(glob_t, loc_t, len_t, ys, lp, h1, g2, b2)


def _work_items(counts, m):
    tm = EXPERT_TILE
    n_tiles = m // tm
    max_items = n_tiles + N_EXPERTS - 1
    grp_end = jnp.cumsum(counts)
    grp_start = grp_end - counts
    first_tile = grp_start // tm
    last_tile = (grp_end - 1) // tm
    n_e = jnp.where(counts > 0, last_tile - first_tile + 1, 0)
    item_end = jnp.cumsum(n_e)
    item_start = item_end - n_e
    total = item_end[-1]
    i = jnp.arange(max_items, dtype=jnp.int32)
    valid = i < total
    ic = jnp.minimum(i, total - 1)
    e = jnp.minimum(jnp.sum(item_end[None, :] <= ic[:, None], axis=1), N_EXPERTS - 1).astype(jnp.int32)
    tile = first_tile[e] + (ic - item_start[e])
    lo = jnp.maximum(grp_start[e], tile * tm) - tile * tm
    hi = jnp.minimum(grp_end[e], (tile + 1) * tm) - tile * tm
    used_tiles = (grp_end[-1] + tm - 1) // tm
    fill_tile = used_tiles + (i - total)
    is_fill = jnp.logical_and(jnp.logical_not(valid), fill_tile < n_tiles)
    tile = jnp.where(valid, tile, jnp.where(is_fill, fill_tile, n_tiles - 1))
    lo = jnp.where(valid, lo, jnp.where(is_fill, 1, 0))
    hi = jnp.where(valid, hi, 0)
    return tile.astype(jnp.int32), e, lo.astype(jnp.int32), hi.astype(jnp.int32), grp_start


def _segment_tables(tab, n_tiles):
    al = SEG_ALIGN
    t3 = tab.reshape(n_tiles, 8, LANES)
    cnt_al = t3[:, 1, :N_EXPERTS].astype(jnp.int32)
    start = t3[:, 2, :N_EXPERTS].astype(jnp.int32)
    grp = jnp.sum(cnt_al, axis=0)
    grp_start = jnp.cumsum(grp) - grp
    ahead = jnp.cumsum(cnt_al, axis=0) - cnt_al
    local = start.reshape(-1) // al
    glob = (grp_start[None, :] + ahead).reshape(-1) // al
    length = cnt_al.reshape(-1) // al
    return grp, local, glob, length, jnp.sum(grp)


def kernel(x, meta_tokens, ln_in_g, ln_in_b, w_in, b_gate, conv_w, conv_b, dt_bias, a_log, d_skip, ssd_norm_w, w_ssd_out, lam_q1, lam_k1, lam_q2, lam_k2, subln_w, w_da_out, w_out, ln1_g, ln1_b, w_router, b_router, w_gate_up, b_gate_up, w_down, b_down, ln2_g, ln2_b):
    bsz, s, d = x.shape
    n = bsz * s
    l = 0
    row = lambda v: v.reshape(1, -1).astype(F32)

    w = w_in[l]
    c0 = SSD_INNER
    c1 = c0 + SSD_CONV_DIM
    c2 = c1 + SSD_HEADS
    c3 = c2 + 3 * D_MODEL
    w_z = w[:, :c0].astype(BF16)
    w_xbc = w[:, c0:c1].astype(BF16)
    w_dt = jnp.pad(w[:, c1:c2], ((0, 0), (0, LANES - SSD_HEADS))).astype(BF16)
    w_qkv = w[:, c2:c3].astype(BF16)
    w_g = w[:, c3:].astype(BF16)
    g0, b0 = row(ln_in_g), row(ln_in_b)

    x2d = x.reshape(n, d)
    q, k, v = _qkv_call(x2d, g0, b0, w_qkv, ROW_TILE)
    _, km, vm = _qkv_call(meta_tokens.astype(F32), g0, b0, w_qkv, N_META)
    km = jnp.pad(km, ((0, LANES - N_META), (0, 0)))
    vm = jnp.pad(vm, ((0, LANES - N_META), (0, 0)))

    lam = (jnp.exp(jnp.sum(lam_q1[l].astype(F32) * lam_k1[l].astype(F32)))
           - jnp.exp(jnp.sum(lam_q2[l].astype(F32) * lam_k2[l].astype(F32))) + LAMBDA_INIT)
    da_n = _attn_call(lam.reshape(1, 1), q.reshape(bsz, s, d), k.reshape(bsz, s, d),
                      v.reshape(bsz, s, d), km, vm, row(subln_w[l]))

    meta_tile = jnp.pad(meta_tokens.astype(F32), ((SSD_TILE - N_META, 0), (0, 0)))
    pad_h = lambda vec: jnp.pad(row(vec), ((0, 0), (0, LANES - SSD_HEADS)))
    ssd_n = _ssd_call(x, meta_tile, g0, b0, w_z, w_xbc, w_dt, conv_w[l].astype(F32), row(conv_b[l]),
                      pad_h(dt_bias[l]), pad_h(a_log[l]),
                      row(jnp.repeat(d_skip[l].astype(F32), SSD_HEAD_DIM)), row(ssd_norm_w[l]))

    w_r = jnp.pad(w_router[l], ((0, 0), (0, LANES - N_EXPERTS))).astype(BF16)
    b_r = jnp.pad(row(b_router[l]), ((0, 0), (0, LANES - N_EXPERTS)))
    h1, hb, lp, tab = _merge_call(
        x2d, ssd_n.reshape(n, d), da_n.reshape(n, d), g0, b0, w_g, row(b_gate[l]),
        w_ssd_out[l].astype(BF16), w_da_out[l].astype(BF16), w_out[l].astype(BF16),
        row(ln1_g[l]), row(ln1_b[l]), w_r, b_r, ROW_TILE)
    n_tiles = n // ROW_TILE
    m_rows = n * TOP_K + n_tiles * N_EXPERTS * SEG_ALIGN
    m_rows = -(-m_rows // EXPERT_TILE) * EXPERT_TILE
    grp, local, glob, length, total = _segment_tables(tab, n_tiles)
    item_tile, item_exp, item_lo, item_hi, _ = _work_items(grp, m_rows)

    tail_len = (-total) % EXPERT_TILE
    fill_dst = total + tail_len + jnp.arange(m_rows // EXPERT_TILE - n * TOP_K // EXPERT_TILE + 1,
                                             dtype=jnp.int32) * EXPERT_TILE
    fill_len = jnp.where(fill_dst + EXPERT_TILE <= m_rows, EXPERT_TILE, 0)
    fdst = jnp.concatenate([total[None], fill_dst]).astype(jnp.int32) // SEG_ALIGN
    flen = jnp.concatenate([tail_len[None], fill_len]).astype(jnp.int32) // SEG_ALIGN
    xs = _sort_call(glob, local, length, fdst, flen, hb, lp, ROW_TILE, m_rows)

    ys = _expert_call(item_tile, item_exp, item_lo, item_hi, xs,
                      w_gate_up[l].astype(F32), b_gate_up[l].reshape(N_EXPERTS, 1, -1).astype(F32),
                      w_down[l].astype(F32), b_down[l].reshape(N_EXPERTS, 1, -1).astype(F32))

    out = _combine_call(glob, local, length, ys, lp, h1, row(ln2_g[l]), row(ln2_b[l]), ROW_TILE)
    return out.reshape(bsz, s, d)
```

```python
import functools
import math

import jax
import jax.numpy as jnp
from jax import lax
from jax.experimental import pallas as pl
from jax.experimental.pallas import tpu as pltpu

F32 = jnp.float32
BF16 = jnp.bfloat16

D_MODEL = 1024
N_META = 16
SSD_HEADS = 16
SSD_HEAD_DIM = 64
SSD_INNER = 1024
SSD_GROUPS = 4
SSD_STATE = 128
SSD_CONV = 4
SSD_CONV_DIM = 2048
DA_HEADS = 8
DA_HEAD_DIM = 64
N_EXPERTS = 32
TOP_K = 4
D_FF = 1024
SWIGLU_LIMIT = 7.0
SWIGLU_ALPHA = 1.702
DEPTH = 1
DEEPNORM_ALPHA = (2.0 * DEPTH) ** 0.25
LN_EPS = 1e-5
RMS_EPS = 1e-6
LAMBDA_INIT = 0.8 - 0.6 * math.exp(-0.3 * 0)
LOG2_E = math.log2(math.e)

LANES = 128
VMEM_LIMIT = 56 * 1024 * 1024

ROW_TILE = 512
ATT_TQ = 512
ATT_TK = 512
ATT_HEADS = 4
SSD_TILE = 256
SSD_CHUNK = 128
EXPERT_TILE = 512
SEG_ALIGN = 8
SEG_BITS = 7
SORT_ROWS = ROW_TILE * 4 + 32 * SEG_ALIGN
SORT_CHUNK = SORT_ROWS // 3
XS_WIDTH = 512 + 128


def _cparams(sem):
    return pltpu.CompilerParams(dimension_semantics=sem, vmem_limit_bytes=VMEM_LIMIT)


def _const_spec(shape):
    nd = len(shape)
    return pl.BlockSpec(shape, lambda *a: (0,) * nd)


def _layer_norm(x, g, b):
    mu = jnp.mean(x, axis=-1, keepdims=True)
    xc = x - mu
    var = jnp.mean(xc * xc, axis=-1, keepdims=True)
    return xc * lax.rsqrt(var + LN_EPS) * g + b


def _sigmoid(x):
    return 1.0 / (1.0 + jnp.exp(-x))


def _dot(a, b):
    return jnp.dot(a, b, preferred_element_type=F32)


def _dot_nt(a, b):
    return lax.dot_general(a, b, (((1,), (1,)), ((), ())), preferred_element_type=F32)


def _qkv_kernel(x_ref, g_ref, b_ref, w_ref, q_ref, k_ref, v_ref):
    h = _layer_norm(x_ref[...], g_ref[...], b_ref[...]).astype(BF16)
    acc = _dot(h, w_ref[...])
    d = D_MODEL
    q_ref[...] = (acc[:, :d] * (DA_HEAD_DIM ** -0.5 * LOG2_E)).astype(BF16)
    k_ref[...] = acc[:, d:2 * d].astype(BF16)
    v_ref[...] = acc[:, 2 * d:].astype(BF16)


def _qkv_call(x2d, g, b, w_qkv, tm):
    n = x2d.shape[0]
    out = jax.ShapeDtypeStruct((n, D_MODEL), BF16)
    row = pl.BlockSpec((tm, D_MODEL), lambda i: (i, 0))
    return pl.pallas_call(
        _qkv_kernel,
        grid=(n // tm,),
        in_specs=[row, _const_spec((1, D_MODEL)), _const_spec((1, D_MODEL)),
                  _const_spec((D_MODEL, 3 * D_MODEL))],
        out_specs=[row, row, row],
        out_shape=[out, out, out],
        compiler_params=_cparams(("parallel",)),
    )(x2d, g, b, w_qkv)


def _attn_kernel(lam_ref, q_ref, k_ref, v_ref, km_ref, vm_ref, sw_ref, o_ref,
                 sa_ref, sb_ref, m_ref, l_ref, acc_ref):
    tq, tk = ATT_TQ, ATT_TK
    i = pl.program_id(2)
    lane = lax.broadcasted_iota(jnp.int32, (1, LANES), 1)
    heads = range(ATT_HEADS)
    hs = [slice(h * LANES, (h + 1) * LANES) for h in heads]

    def stacked_q(h):
        q = q_ref[:, hs[h]]
        zero = jnp.zeros_like(q)
        return jnp.concatenate([jnp.where(lane < DA_HEAD_DIM, q, zero),
                                jnp.where(lane >= DA_HEAD_DIM, q, zero)], axis=0)

    q2 = [stacked_q(h) for h in heads]

    def scores(h, j):
        off = pl.multiple_of(j * tk, tk)
        return _dot_nt(q2[h], k_ref[pl.ds(off, tk), hs[h]])

    def absorb(h, s, j):
        m_old = m_ref[h]
        m_new = jnp.maximum(m_old, jnp.max(s, axis=1, keepdims=True))
        alpha = jnp.exp2(m_old - m_new)
        p = jnp.exp2(s - jnp.concatenate([m_new] * (tk // LANES), axis=1))
        psum = p[:, :LANES]
        for c in range(1, tk // LANES):
            psum = psum + p[:, c * LANES:(c + 1) * LANES]
        l_ref[h] = alpha * l_ref[h] + psum
        off = pl.multiple_of(j * tk, tk)
        acc_ref[h] = alpha * acc_ref[h] + _dot(p.astype(BF16), v_ref[pl.ds(off, tk), hs[h]])
        m_ref[h] = m_new

    meta_ok = lax.broadcasted_iota(jnp.int32, (1, LANES), 1) < N_META
    for h in heads:
        sm = jnp.where(meta_ok, _dot_nt(q2[h], km_ref[:, hs[h]]), -jnp.inf)
        m0 = jnp.broadcast_to(jnp.max(sm, axis=1, keepdims=True), sm.shape)
        p0 = jnp.exp2(sm - m0)
        m_ref[h] = m0
        l_ref[h] = p0
        acc_ref[h] = _dot(p0.astype(BF16), vm_ref[:, hs[h]])

    for h in heads:
        sa_ref[h] = scores(h, 0)

    def body(j, carry):
        @pl.when(j % 2 == 0)
        def _():
            nxt = [scores(h, j + 1) for h in heads]
            for h in heads:
                absorb(h, sa_ref[h], j)
            for h in heads:
                sb_ref[h] = nxt[h]

        @pl.when(j % 2 == 1)
        def _():
            nxt = [scores(h, j + 1) for h in heads]
            for h in heads:
                absorb(h, sb_ref[h], j)
            for h in heads:
                sa_ref[h] = nxt[h]

        return carry

    lax.fori_loop(0, i, body, 0)

    qc = (lax.broadcasted_iota(jnp.int32, (2 * tq, tk), 0) % tq) // 64
    kc = lax.broadcasted_iota(jnp.int32, (2 * tq, tk), 1) // 64
    vis = kc <= qc

    @pl.when(i % 2 == 0)
    def _():
        for h in heads:
            absorb(h, jnp.where(vis, sa_ref[h], -jnp.inf), i)

    @pl.when(i % 2 == 1)
    def _():
        for h in heads:
            absorb(h, jnp.where(vis, sb_ref[h], -jnp.inf), i)

    for h in heads:
        a = acc_ref[h] / jnp.sum(l_ref[h], axis=1, keepdims=True)
        o = a[:tq] - lam_ref[0, 0] * a[tq:]
        o = o * lax.rsqrt(jnp.mean(o * o, axis=-1, keepdims=True) + RMS_EPS) * sw_ref[...]
        o_ref[:, hs[h]] = (o * (1.0 - LAMBDA_INIT)).astype(BF16)


def _attn_call(lam, q, k, v, km, vm, subln_w):
    bsz, s, _ = q.shape
    nq = s // ATT_TQ
    hw = ATT_HEADS * LANES
    qspec = pl.BlockSpec((None, ATT_TQ, hw), lambda b, h, i: (b, i, h))
    kvspec = pl.BlockSpec((None, s, hw), lambda b, h, i: (b, 0, h))
    mspec = pl.BlockSpec((LANES, hw), lambda b, h, i: (0, h))
    rows = 2 * ATT_TQ
    return pl.pallas_call(
        _attn_kernel,
        grid=(bsz, DA_HEADS // ATT_HEADS, nq),
        in_specs=[pl.BlockSpec(memory_space=pltpu.SMEM), qspec, kvspec, kvspec, mspec, mspec,
                  _const_spec((1, LANES))],
        out_specs=qspec,
        out_shape=jax.ShapeDtypeStruct((bsz, s, D_MODEL), BF16),
        scratch_shapes=[pltpu.VMEM((ATT_HEADS, rows, ATT_TK), F32), pltpu.VMEM((ATT_HEADS, rows, ATT_TK), F32),
                        pltpu.VMEM((ATT_HEADS, rows, LANES), F32), pltpu.VMEM((ATT_HEADS, rows, LANES), F32),
                        pltpu.VMEM((ATT_HEADS, rows, LANES), F32)],
        compiler_params=_cparams(("parallel", "parallel", "arbitrary")),
    )(lam, q, k, v, km, vm, subln_w)


def _split3(a):
    hi = a.astype(BF16)
    r = a - hi.astype(F32)
    mid = r.astype(BF16)
    lo = (r - mid.astype(F32)).astype(BF16)
    return hi, mid, lo


def _ssd_kernel(x_ref, meta_ref, g_ref, b_ref, wz_ref, wxbc_ref, wdt_ref, cw_ref, cb_ref,
                dtb_ref, alog_ref, dskip_ref, nw_ref, o_ref, state_ref, cbuf_ref, y_ref):
    lt, lc = SSD_TILE, SSD_CHUNK
    t = pl.program_id(1)
    is_meta = t == 0

    @pl.when(is_meta)
    def _():
        state_ref[...] = jnp.zeros_like(state_ref)
        cbuf_ref[0:8, :] = jnp.zeros((8, SSD_CONV_DIM), F32)

    row = lax.broadcasted_iota(jnp.int32, (lt, 1), 0)
    valid = jnp.logical_or(jnp.logical_not(is_meta), row >= lt - N_META)
    x = jnp.where(is_meta, meta_ref[...], x_ref[...])
    h = _layer_norm(x, g_ref[...], b_ref[...]).astype(BF16)
    z = _dot(h, wz_ref[...])
    xbc = jnp.where(valid, _dot(h, wxbc_ref[...]), 0.0)
    dtr = _dot(h, wdt_ref[...])

    cbuf_ref[8:8 + lt, :] = xbc
    conv = (cw_ref[0:1, :] * cbuf_ref[5:5 + lt, :] + cw_ref[1:2, :] * cbuf_ref[6:6 + lt, :]
            + cw_ref[2:3, :] * cbuf_ref[7:7 + lt, :] + cw_ref[3:4, :] * xbc + cb_ref[...])
    cbuf_ref[0:8, :] = cbuf_ref[lt:lt + 8, :]
    act = jnp.where(valid, conv * _sigmoid(conv), 0.0)
    xs = act[:, :SSD_INNER]
    bm = act[:, SSD_INNER:SSD_INNER + SSD_GROUPS * SSD_STATE]
    cm = act[:, SSD_INNER + SSD_GROUPS * SSD_STATE:].astype(BF16)
    bm_t = bm.T.astype(BF16)

    dtv = dtr + dtb_ref[...]
    dt = jnp.maximum(dtv, 0.0) + jnp.log1p(jnp.exp(-jnp.abs(dtv)))
    dt = jnp.where(valid, dt, 0.0)
    a = dt * (-jnp.exp(alog_ref[...]))

    lane = lax.broadcasted_iota(jnp.int32, (1, LANES), 1)
    left = lane < SSD_HEAD_DIM
    tri_r = lax.broadcasted_iota(jnp.int32, (lc, lc), 0)
    tri_c = lax.broadcasted_iota(jnp.int32, (lc, lc), 1)
    causal = tri_c <= tri_r
    tri = jnp.where(causal, 1.0, 0.0).astype(BF16)

    for c in range(lt // lc):
        rs = slice(c * lc, (c + 1) * lc)
        hi, mid, lo = _split3(a[rs])
        acs = _dot(tri, hi) + _dot(tri, mid) + _dot(tri, lo)
        acs_t = acs.T
        dt_c = dt[rs]
        for g in range(SSD_GROUPS):
            c_g = cm[rs, g * SSD_STATE:(g + 1) * SSD_STATE]
            bt_g = bm_t[g * SSD_STATE:(g + 1) * SSD_STATE, rs]
            cb = _dot(c_g, bt_g)
            for jj in range(2):
                j = 2 * g + jj
                h0, h1 = 2 * j, 2 * j + 1
                col0, col1 = acs[:, h0:h0 + 1], acs[:, h1:h1 + 1]
                row0, row1 = acs_t[h0:h0 + 1, :], acs_t[h1:h1 + 1, :]
                l0 = jnp.exp(jnp.where(causal, col0 - row0, -jnp.inf))
                l1 = jnp.exp(jnp.where(causal, col1 - row1, -jnp.inf))
                mm = jnp.concatenate([(cb * l0).astype(BF16), (cb * l1).astype(BF16)], axis=0)
                xp = xs[rs, j * LANES:(j + 1) * LANES]
                xdt = xp * jnp.where(left, dt_c[:, h0:h0 + 1], dt_c[:, h1:h1 + 1])
                yy = _dot(mm, xdt.astype(BF16))
                y_diag = jnp.where(left, yy[:lc], yy[lc:])
                acs_p = jnp.where(left, col0, col1)
                st = state_ref[j]
                y_off = _dot(c_g, st.astype(BF16)) * jnp.exp(acs_p)
                last_p = jnp.where(left, acs[lc - 1:lc, h0:h0 + 1], acs[lc - 1:lc, h1:h1 + 1])
                xd = (xdt * jnp.exp(last_p - acs_p)).astype(BF16)
                state_ref[j] = st * jnp.exp(last_p) + _dot(bt_g, xd)
                y_ref[rs, j * LANES:(j + 1) * LANES] = (
                    y_diag + y_off + xp * dskip_ref[:, j * LANES:(j + 1) * LANES])

    gy = y_ref[...] * (z * _sigmoid(z))
    gw = SSD_INNER // SSD_GROUPS
    outs = []
    for g in range(SSD_GROUPS):
        gg = gy[:, g * gw:(g + 1) * gw]
        outs.append(gg * lax.rsqrt(jnp.mean(gg * gg, axis=-1, keepdims=True) + RMS_EPS))
    o_ref[...] = (jnp.concatenate(outs, axis=1) * nw_ref[...]).astype(BF16)


def _ssd_call(x3, meta_tile, g, b, wz, wxbc, wdt, cw, cb, dtb, alog, dskip, nw):
    bsz, s, _ = x3.shape
    lt = SSD_TILE
    nt = s // lt
    xspec = pl.BlockSpec((None, lt, D_MODEL), lambda bb, t: (bb, jnp.maximum(t - 1, 0), 0))
    return pl.pallas_call(
        _ssd_kernel,
        grid=(bsz, nt + 1),
        in_specs=[xspec, _const_spec((lt, D_MODEL)), _const_spec((1, D_MODEL)),
                  _const_spec((1, D_MODEL)), _const_spec((D_MODEL, SSD_INNER)),
                  _const_spec((D_MODEL, SSD_CONV_DIM)), _const_spec((D_MODEL, LANES)),
                  _const_spec((SSD_CONV, SSD_CONV_DIM)), _const_spec((1, SSD_CONV_DIM)),
                  _const_spec((1, LANES)), _const_spec((1, LANES)), _const_spec((1, SSD_INNER)),
                  _const_spec((1, SSD_INNER))],
        out_specs=xspec,
        out_shape=jax.ShapeDtypeStruct((bsz, s, SSD_INNER), BF16),
        scratch_shapes=[pltpu.VMEM((SSD_HEADS // 2, SSD_STATE, LANES), F32),
                        pltpu.VMEM((lt + 8, SSD_CONV_DIM), F32),
                        pltpu.VMEM((lt, SSD_INNER), F32)],
        compiler_params=_cparams(("parallel", "arbitrary")),
    )(x3, meta_tile, g, b, wz, wxbc, wdt, cw, cb, dtb, alog, dskip, nw)


def _merge_kernel(x_ref, ssd_ref, da_ref, g0_ref, b0_ref, wg_ref, bg_ref, wso_ref, wdo_ref, wo_ref,
                  g1_ref, b1_ref, wr_ref, br_ref, h1_ref, hb_ref, lp_ref, lpt_ref, tab_ref):
    tm = x_ref.shape[0]
    d = D_MODEL

    h = _layer_norm(x_ref[...], g0_ref[...], b0_ref[...])
    gates = _sigmoid(_dot(h.astype(BF16), wg_ref[...]) + bg_ref[...])
    y_ssd = _dot(ssd_ref[...], wso_ref[...])
    y_da = _dot(da_ref[...], wdo_ref[...])
    merged = gates[:, :d] * y_ssd + gates[:, d:] * y_da
    mix = _dot(merged.astype(BF16), wo_ref[...])
    h1 = _layer_norm(DEEPNORM_ALPHA * h + mix, g1_ref[...], b1_ref[...])
    h1_ref[...] = h1
    hb = h1.astype(BF16)
    hb_ref[...] = hb

    logits = (_dot_nt(wr_ref[...], hb) + br_ref[...])[:N_EXPERTS]
    e_io = lax.broadcasted_iota(jnp.int32, (N_EXPERTS, tm), 0)
    vals, sels = [], []
    for k in range(TOP_K):
        mx = jnp.max(logits, axis=0, keepdims=True)
        idx = jnp.min(jnp.where(logits == mx, e_io, N_EXPERTS), axis=0, keepdims=True)
        sel = e_io == idx
        logits = jnp.where(sel, -jnp.inf, logits)
        vals.append(mx)
        sels.append(sel)
    exps = [jnp.exp(v - vals[0]) for v in vals]
    den = exps[0] + exps[1] + exps[2] + exps[3]

    onehot = jnp.where(sels[0] | sels[1] | sels[2] | sels[3], 1.0, 0.0)
    onehot_p = jnp.concatenate([onehot, jnp.zeros((LANES - N_EXPERTS, tm), F32)], axis=0).astype(BF16)
    r_i = lax.broadcasted_iota(jnp.int32, (tm, tm), 0)
    c_i = lax.broadcasted_iota(jnp.int32, (tm, tm), 1)
    earlier = jnp.where(r_i < c_i, 1.0, 0.0).astype(BF16)
    before = _dot(onehot_p, earlier)[:N_EXPERTS]
    cnt = jnp.sum(onehot_p.astype(F32), axis=1, keepdims=True)
    cnt_al = jnp.floor((cnt + (SEG_ALIGN - 1)) * (1.0 / SEG_ALIGN)) * SEG_ALIGN
    e_r = lax.broadcasted_iota(jnp.int32, (LANES, LANES), 0)
    e_c = lax.broadcasted_iota(jnp.int32, (LANES, LANES), 1)
    lower = jnp.where(e_c < e_r, 1.0, 0.0).astype(BF16)
    start = _dot(lower, jnp.broadcast_to(cnt_al, (LANES, LANES)).astype(BF16))[:, 0:1]
    pos = before + start[:N_EXPERTS]
    rows = [jnp.sum(jnp.where(sels[k], pos, 0.0), axis=0, keepdims=True) for k in range(TOP_K)]
    rows += [exps[k] / den for k in range(TOP_K)]
    lp_t = jnp.concatenate(rows, axis=0)
    lpt_ref[...] = lp_t
    lp_ref[...] = jnp.concatenate([lp_t, jnp.zeros((LANES - 2 * TOP_K, tm), F32)], axis=0).T
    cols = jnp.where(e_c == 0, cnt, jnp.where(e_c == 1, cnt_al, jnp.where(e_c == 2, start, 0.0)))
    tab_ref[...] = cols.T[:8]


def _merge_call(x2d, ssd_n, da_n, g0, b0, wg, bg, wso, wdo, wo, g1, b1, wr, br, tm):
    n = x2d.shape[0]
    d = D_MODEL
    rowf = pl.BlockSpec((tm, d), lambda i: (i, 0))
    rowl = pl.BlockSpec((tm, LANES), lambda i: (i, 0))
    return pl.pallas_call(
        _merge_kernel,
        grid=(n // tm,),
        in_specs=[rowf, rowf, rowf, _const_spec((1, d)), _const_spec((1, d)),
                  _const_spec((d, 2 * d)), _const_spec((1, 2 * d)), _const_spec((d, d)),
                  _const_spec((d, d)), _const_spec((d, d)), _const_spec((1, d)), _const_spec((1, d)),
                  _const_spec((LANES, d)), _const_spec((LANES, 1))],
        out_specs=[rowf, rowf, rowl, pl.BlockSpec((2 * TOP_K, tm), lambda i: (0, i)),
                   pl.BlockSpec((8, LANES), lambda i: (i, 0))],
        out_shape=[jax.ShapeDtypeStruct((n, d), F32), jax.ShapeDtypeStruct((n, d), BF16),
                   jax.ShapeDtypeStruct((n, LANES), F32), jax.ShapeDtypeStruct((2 * TOP_K, n), F32),
                   jax.ShapeDtypeStruct((n // tm * 8, LANES), F32)],
        compiler_params=_cparams(("parallel",)),
    )(x2d, ssd_n, da_n, g0, b0, wg, bg, wso, wdo, wo, g1, b1, wr, br)


def _run_copy(src_ref, dst_ref, s_al, d_al, n_al, sem, wait):
    for b in reversed(range(SEG_BITS)):
        size = (1 << b) * SEG_ALIGN
        done = (n_al >> (b + 1)) << (b + 1)

        @pl.when((n_al & (1 << b)) != 0)
        def _():
            s0 = pl.multiple_of((s_al + done) * SEG_ALIGN, SEG_ALIGN)
            d0 = pl.multiple_of((d_al + done) * SEG_ALIGN, SEG_ALIGN)
            cp = pltpu.make_async_copy(src_ref.at[pl.ds(s0, size), :], dst_ref.at[pl.ds(d0, size), :], sem)
            if wait:
                cp.wait()
            else:
                cp.start()


def _hits(lp_t, k, rows, r0):
    r = lax.broadcasted_iota(jnp.int32, (rows, lp_t.shape[1]), 0) + r0
    return r == lp_t[k:k + 1, :].astype(jnp.int32)


def _sort_kernel(glob_t, loc_t, len_t, fdst_t, flen_t, hb_ref, lpt_ref, xs_ref, buf_ref, zero_ref, sems):
    d = D_MODEL
    i = pl.program_id(0)
    last = pl.num_programs(0) - 1

    def sort_into(slot):
        lp_t = lpt_ref[...]
        hb = hb_ref[...]
        for c in range(SORT_ROWS // SORT_CHUNK):
            r0 = c * SORT_CHUNK
            hits = [_hits(lp_t, k, SORT_CHUNK, r0) for k in range(TOP_K)]
            sel = jnp.where(hits[0] | hits[1] | hits[2] | hits[3], 1.0, 0.0).astype(BF16)
            xsort = _dot(sel, hb)
            bits = pltpu.bitcast(xsort, jnp.uint32)
            buf_ref[slot, r0:r0 + SORT_CHUNK, :d // 2] = bits[:, :d // 2] | (bits[:, d // 2:] >> 16)
            g = jnp.where(hits[0], lp_t[TOP_K:TOP_K + 1, :], 0.0)
            for k in range(1, TOP_K):
                g = g + jnp.where(hits[k], lp_t[TOP_K + k:TOP_K + k + 1, :], 0.0)
            gsum = jnp.sum(g, axis=1, keepdims=True)
            buf_ref[slot, r0:r0 + SORT_CHUNK, d // 2:] = pltpu.bitcast(
                jnp.broadcast_to(gsum, (SORT_CHUNK, LANES)), jnp.uint32)

    def runs(tile, slot, wait):
        def one(e, carry):
            s = tile * N_EXPERTS + e
            _run_copy(buf_ref.at[slot], xs_ref, loc_t[s], glob_t[s], len_t[s], sems.at[slot], wait)
            return carry

        lax.fori_loop(0, N_EXPERTS, one, 0)

    def fills(wait):
        def one(k, carry):
            _run_copy(zero_ref, xs_ref, 0, fdst_t[k], flen_t[k], sems.at[2], wait)
            return carry

        lax.fori_loop(0, fdst_t.shape[0], one, 0)

    for slot in range(2):
        @pl.when(i % 2 == slot)
        def _():
            sort_into(slot)
            runs(i, slot, False)

            @pl.when(i > 0)
            def _():
                runs(i - 1, 1 - slot, True)

            @pl.when(i == last)
            def _():
                zero_ref[...] = jnp.zeros_like(zero_ref)
                fills(False)
                runs(i, slot, True)
                fills(True)


def _sort_call(glob_t, loc_t, len_t, fdst_t, flen_t, hb, lpt, tm, out_rows):
    n, d = hb.shape
    grid_spec = pltpu.PrefetchScalarGridSpec(
        num_scalar_prefetch=5,
        grid=(n // tm,),
        in_specs=[pl.BlockSpec((tm, d), lambda i, *_: (i, 0)),
                  pl.BlockSpec((2 * TOP_K, tm), lambda i, *_: (0, i))],
        out_specs=pl.BlockSpec(memory_space=pl.ANY),
        scratch_shapes=[pltpu.VMEM((2, SORT_ROWS, XS_WIDTH), jnp.uint32),
                        pltpu.VMEM((EXPERT_TILE, XS_WIDTH), jnp.uint32),
                        pltpu.SemaphoreType.DMA((3,))],
    )
    return pl.pallas_call(
        _sort_kernel,
        grid_spec=grid_spec,
        out_shape=jax.ShapeDtypeStruct((out_rows, XS_WIDTH), jnp.uint32),
        compiler_params=_cparams(("arbitrary",)),
    )(glob_t, loc_t, len_t, fdst_t, flen_t, hb, lpt)


def _expert_kernel(tile_ref, exp_ref, lo_ref, hi_ref, xs_ref, wgu_ref, bgu_ref, wd_ref, bd_ref, ys_ref,
                   wgu_bf_ref, wd_bf_ref):
    i = pl.program_id(0)
    lo = lo_ref[i]
    hi = hi_ref[i]
    tm = xs_ref.shape[0]
    d = D_MODEL

    @pl.when(jnp.logical_or(i == 0, exp_ref[i] != exp_ref[jnp.maximum(i - 1, 0)]))
    def _():
        wgu_bf_ref[...] = wgu_ref[...].astype(BF16)
        wd_bf_ref[...] = wd_ref[...].astype(BF16)

    @pl.when(lo > hi)
    def _():
        ys_ref[...] = jnp.zeros_like(ys_ref)

    @pl.when(lo < hi)
    def _():
        w = xs_ref[:, :d // 2]
        x_hi = pltpu.bitcast(w & jnp.uint32(0xFFFF0000), F32).astype(BF16)
        x_lo = pltpu.bitcast(w << 16, F32).astype(BF16)
        x = jnp.concatenate([x_hi, x_lo], axis=1)
        hid = _dot(x, wgu_bf_ref[...]) + bgu_ref[...]
        gate = jnp.minimum(hid[:, :D_FF], SWIGLU_LIMIT)
        up = jnp.clip(hid[:, D_FF:], -SWIGLU_LIMIT, SWIGLU_LIMIT)
        act = gate * _sigmoid(SWIGLU_ALPHA * gate) * (up + 1.0)
        y = _dot(act.astype(BF16), wd_bf_ref[...]) + bd_ref[...]
        rgate = pltpu.bitcast(xs_ref[:, d // 2:], F32)
        y = y * jnp.concatenate([rgate] * (d // LANES), axis=1)
        row = lax.broadcasted_iota(jnp.int32, (tm, 1), 0)
        mine = jnp.logical_and(row >= lo, row < hi)

        @pl.when(lo == 0)
        def _():
            ys_ref[...] = jnp.where(mine, y, 0.0)

        @pl.when(lo > 0)
        def _():
            ys_ref[...] = jnp.where(mine, y, ys_ref[...])


def _expert_call(item_tile, item_exp, item_lo, item_hi, xs, wgu, bgu, wd, bd):
    m, w = xs.shape
    tm = EXPERT_TILE
    n_items = item_tile.shape[0]
    grid_spec = pltpu.PrefetchScalarGridSpec(
        num_scalar_prefetch=4,
        grid=(n_items,),
        in_specs=[pl.BlockSpec((tm, w), lambda i, t, e, lo, hi: (t[i], 0)),
                  pl.BlockSpec((None, D_MODEL, 2 * D_FF), lambda i, t, e, lo, hi: (e[i], 0, 0)),
                  pl.BlockSpec((None, 1, 2 * D_FF), lambda i, t, e, lo, hi: (e[i], 0, 0)),
                  pl.BlockSpec((None, D_FF, D_MODEL), lambda i, t, e, lo, hi: (e[i], 0, 0)),
                  pl.BlockSpec((None, 1, D_MODEL), lambda i, t, e, lo, hi: (e[i], 0, 0))],
        out_specs=pl.BlockSpec((tm, D_MODEL), lambda i, t, e, lo, hi: (t[i], 0)),
        scratch_shapes=[pltpu.VMEM((D_MODEL, 2 * D_FF), BF16), pltpu.VMEM((D_FF, D_MODEL), BF16)],
    )
    return pl.pallas_call(
        _expert_kernel,
        grid_spec=grid_spec,
        out_shape=jax.ShapeDtypeStruct((m, D_MODEL), F32),
        compiler_params=_cparams(("arbitrary",)),
    )(item_tile, item_exp, item_lo, item_hi, xs, wgu, bgu, wd, bd)


def _combine_kernel(glob_t, loc_t, len_t, ys_ref, lp_ref, h1_ref, g2_ref, b2_ref, o_ref, ybuf_ref, sems):
    tm = h1_ref.shape[0]
    i = pl.program_id(0)
    n_steps = pl.num_programs(0)

    def fetch(tile, slot, wait):
        if not wait:
            ybuf_ref[slot, TOP_K * tm:, :] = jnp.zeros((SORT_ROWS - TOP_K * tm, D_MODEL), F32)

        def one(e, carry):
            s = tile * N_EXPERTS + e
            _run_copy(ys_ref, ybuf_ref.at[slot], glob_t[s], loc_t[s], len_t[s], sems.at[slot], wait)
            return carry

        lax.fori_loop(0, N_EXPERTS, one, 0)

    def reduce_from(slot):
        lp = lp_ref[...]
        ffn = jnp.zeros((tm, D_MODEL), F32)
        for c in range(SORT_ROWS // SORT_CHUNK):
            r0 = c * SORT_CHUNK
            r = lax.broadcasted_iota(jnp.int32, (tm, SORT_CHUNK), 1) + r0
            sel = jnp.where(r == lp[:, 0:1].astype(jnp.int32), 1.0, 0.0)
            for k in range(1, TOP_K):
                sel = sel + jnp.where(r == lp[:, k:k + 1].astype(jnp.int32), 1.0, 0.0)
            sel = sel.astype(BF16)
            y = ybuf_ref[slot, r0:r0 + SORT_CHUNK, :]
            y_hi = y.astype(BF16)
            y_lo = (y - y_hi.astype(F32)).astype(BF16)
            ffn = ffn + _dot(sel, y_hi) + _dot(sel, y_lo)
        o_ref[...] = _layer_norm(DEEPNORM_ALPHA * h1_ref[...] + ffn, g2_ref[...], b2_ref[...])

    @pl.when(i == 0)
    def _():
        fetch(0, 0, False)

    for slot in range(2):
        @pl.when(i % 2 == slot)
        def _():
            @pl.when(i + 1 < n_steps)
            def _():
                fetch(i + 1, 1 - slot, False)

            fetch(i, slot, True)
            reduce_from(slot)


def _combine_call(glob_t, loc_t, len_t, ys, lp, h1, g2, b2, tm):
    n, d = h1.shape
    rowf = pl.BlockSpec((tm, d), lambda i, *_: (i, 0))
    grid_spec = pltpu.PrefetchScalarGridSpec(
        num_scalar_prefetch=3,
        grid=(n // tm,),
        in_specs=[pl.BlockSpec(memory_space=pl.ANY), pl.BlockSpec((tm, LANES), lambda i, *_: (i, 0)),
                  rowf, pl.BlockSpec((1, d), lambda i, *_: (0, 0)), pl.BlockSpec((1, d), lambda i, *_: (0, 0))],
        out_specs=rowf,
        scratch_shapes=[pltpu.VMEM((2, SORT_ROWS, d), F32), pltpu.SemaphoreType.DMA((2,))],
    )
    return pl.pallas_call(
        _combine_kernel,
        grid_spec=grid_spec,
        out_shape=jax.ShapeDtypeStruct((n, d), F32),
        compiler_params=_cparams(("arbitrary",)),
    )(glob_t, loc_t, len_t, ys, lp, h1, g2, b2)


def _work_items(counts, m):
    tm = EXPERT_TILE
    n_tiles = m // tm
    max_items = n_tiles + N_EXPERTS - 1
    grp_end = jnp.cumsum(counts)
    grp_start = grp_end - counts
    first_tile = grp_start // tm
    last_tile = (grp_end - 1) // tm
    n_e = jnp.where(counts > 0, last_tile - first_tile + 1, 0)
    item_end = jnp.cumsum(n_e)
    item_start = item_end - n_e
    total = item_end[-1]
    i = jnp.arange(max_items, dtype=jnp.int32)
    valid = i < total
    ic = jnp.minimum(i, total - 1)
    e = jnp.minimum(jnp.sum(item_end[None, :] <= ic[:, None], axis=1), N_EXPERTS - 1).astype(jnp.int32)
    tile = first_tile[e] + (ic - item_start[e])
    lo = jnp.maximum(grp_start[e], tile * tm) - tile * tm
    hi = jnp.minimum(grp_end[e], (tile + 1) * tm) - tile * tm
    used_tiles = (grp_end[-1] + tm - 1) // tm
    fill_tile = used_tiles + (i - total)
    is_fill = jnp.logical_and(jnp.logical_not(valid), fill_tile < n_tiles)
    tile = jnp.where(valid, tile, jnp.where(is_fill, fill_tile, n_tiles - 1))
    lo = jnp.where(valid, lo, jnp.where(is_fill, 1, 0))
    hi = jnp.where(valid, hi, 0)
    return tile.astype(jnp.int32), e, lo.astype(jnp.int32), hi.astype(jnp.int32), grp_start


def _segment_tables(tab, n_tiles):
    al = SEG_ALIGN
    t3 = tab.reshape(n_tiles, 8, LANES)
    cnt_al = t3[:, 1, :N_EXPERTS].astype(jnp.int32)
    start = t3[:, 2, :N_EXPERTS].astype(jnp.int32)
    grp = jnp.sum(cnt_al, axis=0)
    grp_start = jnp.cumsum(grp) - grp
    ahead = jnp.cumsum(cnt_al, axis=0) - cnt_al
    local = start.reshape(-1) // al
    glob = (grp_start[None, :] + ahead).reshape(-1) // al
    length = cnt_al.reshape(-1) // al
    return grp, local, glob, length, jnp.sum(grp)


def kernel(x, meta_tokens, ln_in_g, ln_in_b, w_in, b_gate, conv_w, conv_b, dt_bias, a_log, d_skip, ssd_norm_w, w_ssd_out, lam_q1, lam_k1, lam_q2, lam_k2, subln_w, w_da_out, w_out, ln1_g, ln1_b, w_router, b_router, w_gate_up, b_gate_up, w_down, b_down, ln2_g, ln2_b):
    bsz, s, d = x.shape
    n = bsz * s
    l = 0
    row = lambda v: v.reshape(1, -1).astype(F32)

    w = w_in[l]
    c0 = SSD_INNER
    c1 = c0 + SSD_CONV_DIM
    c2 = c1 + SSD_HEADS
    c3 = c2 + 3 * D_MODEL
    w_z = w[:, :c0].astype(BF16)
    w_xbc = w[:, c0:c1].astype(BF16)
    w_dt = jnp.pad(w[:, c1:c2], ((0, 0), (0, LANES - SSD_HEADS))).astype(BF16)
    w_qkv = w[:, c2:c3].astype(BF16)
    w_g = w[:, c3:].astype(BF16)
    g0, b0 = row(ln_in_g), row(ln_in_b)

    x2d = x.reshape(n, d)
    q, k, v = _qkv_call(x2d, g0, b0, w_qkv, ROW_TILE)
    _, km, vm = _qkv_call(meta_tokens.astype(F32), g0, b0, w_qkv, N_META)
    km = jnp.pad(km, ((0, LANES - N_META), (0, 0)))
    vm = jnp.pad(vm, ((0, LANES - N_META), (0, 0)))

    lam = (jnp.exp(jnp.sum(lam_q1[l].astype(F32) * lam_k1[l].astype(F32)))
           - jnp.exp(jnp.sum(lam_q2[l].astype(F32) * lam_k2[l].astype(F32))) + LAMBDA_INIT)
    da_n = _attn_call(lam.reshape(1, 1), q.reshape(bsz, s, d), k.reshape(bsz, s, d),
                      v.reshape(bsz, s, d), km, vm, row(subln_w[l]))

    meta_tile = jnp.pad(meta_tokens.astype(F32), ((SSD_TILE - N_META, 0), (0, 0)))
    pad_h = lambda vec: jnp.pad(row(vec), ((0, 0), (0, LANES - SSD_HEADS)))
    ssd_n = _ssd_call(x, meta_tile, g0, b0, w_z, w_xbc, w_dt, conv_w[l].astype(F32), row(conv_b[l]),
                      pad_h(dt_bias[l]), pad_h(a_log[l]),
                      row(jnp.repeat(d_skip[l].astype(F32), SSD_HEAD_DIM)), row(ssd_norm_w[l]))

    w_r = jnp.pad(w_router[l].T, ((0, LANES - N_EXPERTS), (0, 0))).astype(BF16)
    b_r = jnp.pad(b_router[l].astype(F32).reshape(-1, 1), ((0, LANES - N_EXPERTS), (0, 0)))
    h1, hb, lp, lpt, tab = _merge_call(
        x2d, ssd_n.reshape(n, d), da_n.reshape(n, d), g0, b0, w_g, row(b_gate[l]),
        w_ssd_out[l].astype(BF16), w_da_out[l].astype(BF16), w_out[l].astype(BF16),
        row(ln1_g[l]), row(ln1_b[l]), w_r, b_r, ROW_TILE)
    n_tiles = n // ROW_TILE
    m_rows = n * TOP_K + n_tiles * N_EXPERTS * SEG_ALIGN
    m_rows = -(-m_rows // EXPERT_TILE) * EXPERT_TILE
    grp, local, glob, length, total = _segment_tables(tab, n_tiles)
    item_tile, item_exp, item_lo, item_hi, _ = _work_items(grp, m_rows)

    tail_len = (-total) % EXPERT_TILE
    fill_dst = total + tail_len + jnp.arange(m_rows // EXPERT_TILE - n * TOP_K // EXPERT_TILE + 1,
                                             dtype=jnp.int32) * EXPERT_TILE
    fill_len = jnp.where(fill_dst + EXPERT_TILE <= m_rows, EXPERT_TILE, 0)
    fdst = jnp.concatenate([total[None], fill_dst]).astype(jnp.int32) // SEG_ALIGN
    flen = jnp.concatenate([tail_len[None], fill_len]).astype(jnp.int32) // SEG_ALIGN
    xs = _sort_call(glob, local, length, fdst, flen, hb, lpt, ROW_TILE, m_rows)

    ys = _expert_call(item_tile, item_exp, item_lo, item_hi, xs,
                      w_gate_up[l].astype(F32), b_gate_up[l].reshape(N_EXPERTS, 1, -1).astype(F32),
                      w_down[l].astype(F32), b_down[l].reshape(N_EXPERTS, 1, -1).astype(F32))

    out = _combine_call(glob, local, length, ys, lp, h1, row(ln2_g[l]), row(ln2_b[l]), ROW_TILE)
    return out.reshape(bsz, s, d)
```

```python
import functools
import math

import jax
import jax.numpy as jnp
from jax import lax
from jax.experimental import pallas as pl
from jax.experimental.pallas import tpu as pltpu

F32 = jnp.float32
BF16 = jnp.bfloat16

D_MODEL = 1024
N_META = 16
SSD_HEADS = 16
SSD_HEAD_DIM = 64
SSD_INNER = 1024
SSD_GROUPS = 4
SSD_STATE = 128
SSD_CONV = 4
SSD_CONV_DIM = 2048
DA_HEADS = 8
DA_HEAD_DIM = 64
N_EXPERTS = 32
TOP_K = 4
D_FF = 1024
SWIGLU_LIMIT = 7.0
SWIGLU_ALPHA = 1.702
DEPTH = 1
DEEPNORM_ALPHA = (2.0 * DEPTH) ** 0.25
LN_EPS = 1e-5
RMS_EPS = 1e-6
LAMBDA_INIT = 0.8 - 0.6 * math.exp(-0.3 * 0)
LOG2_E = math.log2(math.e)

LANES = 128
VMEM_LIMIT = 56 * 1024 * 1024

ROW_TILE = 512
ATT_TQ = 512
ATT_TK = 512
ATT_HEADS = 4
SSD_TILE = 256
SSD_CHUNK = 128
EXPERT_TILE = 512
SEG_ALIGN = 8
SEG_BITS = 7
SORT_ROWS = ROW_TILE * 4 + 32 * SEG_ALIGN
SORT_CHUNK = SORT_ROWS // 3
XS_WIDTH = 512 + 128


def _cparams(sem):
    return pltpu.CompilerParams(dimension_semantics=sem, vmem_limit_bytes=VMEM_LIMIT)


def _const_spec(shape):
    nd = len(shape)
    return pl.BlockSpec(shape, lambda *a: (0,) * nd)


def _layer_norm(x, g, b):
    mu = jnp.mean(x, axis=-1, keepdims=True)
    xc = x - mu
    var = jnp.mean(xc * xc, axis=-1, keepdims=True)
    return xc * lax.rsqrt(var + LN_EPS) * g + b


def _sigmoid(x):
    return 1.0 / (1.0 + jnp.exp(-x))


def _dot(a, b):
    return jnp.dot(a, b, preferred_element_type=F32)


def _dot_nt(a, b):
    return lax.dot_general(a, b, (((1,), (1,)), ((), ())), preferred_element_type=F32)


def _qkv_kernel(x_ref, g_ref, b_ref, w_ref, q_ref, k_ref, v_ref):
    h = _layer_norm(x_ref[...], g_ref[...], b_ref[...]).astype(BF16)
    acc = _dot(h, w_ref[...])
    d = D_MODEL
    q_ref[...] = (acc[:, :d] * (DA_HEAD_DIM ** -0.5 * LOG2_E)).astype(BF16)
    k_ref[...] = acc[:, d:2 * d].astype(BF16)
    v_ref[...] = acc[:, 2 * d:].astype(BF16)


def _qkv_call(x2d, g, b, w_qkv, tm):
    n = x2d.shape[0]
    out = jax.ShapeDtypeStruct((n, D_MODEL), BF16)
    row = pl.BlockSpec((tm, D_MODEL), lambda i: (i, 0))
    return pl.pallas_call(
        _qkv_kernel,
        grid=(n // tm,),
        in_specs=[row, _const_spec((1, D_MODEL)), _const_spec((1, D_MODEL)),
                  _const_spec((D_MODEL, 3 * D_MODEL))],
        out_specs=[row, row, row],
        out_shape=[out, out, out],
        compiler_params=_cparams(("parallel",)),
    )(x2d, g, b, w_qkv)


def _attn_kernel(lam_ref, q_ref, k_ref, v_ref, km_ref, vm_ref, sw_ref, o_ref,
                 sa_ref, sb_ref, m_ref, l_ref, acc_ref):
    tq, tk = ATT_TQ, ATT_TK
    i = pl.program_id(2)
    lane = lax.broadcasted_iota(jnp.int32, (1, LANES), 1)
    heads = range(ATT_HEADS)
    hs = [slice(h * LANES, (h + 1) * LANES) for h in heads]

    def stacked_q(h):
        q = q_ref[:, hs[h]]
        zero = jnp.zeros_like(q)
        return jnp.concatenate([jnp.where(lane < DA_HEAD_DIM, q, zero),
                                jnp.where(lane >= DA_HEAD_DIM, q, zero)], axis=0)

    q2 = [stacked_q(h) for h in heads]

    def scores(h, j):
        off = pl.multiple_of(j * tk, tk)
        return _dot_nt(q2[h], k_ref[pl.ds(off, tk), hs[h]])

    def absorb(h, s, j):
        m_old = m_ref[h]
        m_new = jnp.maximum(m_old, jnp.max(s, axis=1, keepdims=True))
        alpha = jnp.exp2(m_old - m_new)
        p = jnp.exp2(s - jnp.concatenate([m_new] * (tk // LANES), axis=1))
        psum = p[:, :LANES]
        for c in range(1, tk // LANES):
            psum = psum + p[:, c * LANES:(c + 1) * LANES]
        l_ref[h] = alpha * l_ref[h] + psum
        off = pl.multiple_of(j * tk, tk)
        acc_ref[h] = alpha * acc_ref[h] + _dot(p.astype(BF16), v_ref[pl.ds(off, tk), hs[h]])
        m_ref[h] = m_new

    for h in heads:
        sa_ref[h] = scores(h, 0)

    meta_ok = lax.broadcasted_iota(jnp.int32, (1, LANES), 1) < N_META
    for h in heads:
        sm = jnp.where(meta_ok, _dot_nt(q2[h], km_ref[:, hs[h]]), -jnp.inf)
        m0 = jnp.broadcast_to(jnp.max(sm, axis=1, keepdims=True), sm.shape)
        p0 = jnp.exp2(sm - m0)
        m_ref[h] = m0
        l_ref[h] = p0
        acc_ref[h] = _dot(p0.astype(BF16), vm_ref[:, hs[h]])

    def body(j, carry):
        @pl.when(j % 2 == 0)
        def _():
            nxt = [scores(h, j + 1) for h in heads]
            for h in heads:
                absorb(h, sa_ref[h], j)
            for h in heads:
                sb_ref[h] = nxt[h]

        @pl.when(j % 2 == 1)
        def _():
            nxt = [scores(h, j + 1) for h in heads]
            for h in heads:
                absorb(h, sb_ref[h], j)
            for h in heads:
                sa_ref[h] = nxt[h]

        return carry

    lax.fori_loop(0, i, body, 0)

    qc = (lax.broadcasted_iota(jnp.int32, (2 * tq, tk), 0) % tq) // 64
    kc = lax.broadcasted_iota(jnp.int32, (2 * tq, tk), 1) // 64
    vis = kc <= qc

    @pl.when(i % 2 == 0)
    def _():
        for h in heads:
            absorb(h, jnp.where(vis, sa_ref[h], -jnp.inf), i)

    @pl.when(i % 2 == 1)
    def _():
        for h in heads:
            absorb(h, jnp.where(vis, sb_ref[h], -jnp.inf), i)

    for h in heads:
        a = acc_ref[h] / jnp.sum(l_ref[h], axis=1, keepdims=True)
        o = a[:tq] - lam_ref[0, 0] * a[tq:]
        o = o * lax.rsqrt(jnp.mean(o * o, axis=-1, keepdims=True) + RMS_EPS) * sw_ref[...]
        o_ref[:, hs[h]] = (o * (1.0 - LAMBDA_INIT)).astype(BF16)


def _attn_call(lam, q, k, v, km, vm, subln_w):
    bsz, s, _ = q.shape
    nq = s // ATT_TQ
    hw = ATT_HEADS * LANES
    qspec = pl.BlockSpec((None, ATT_TQ, hw), lambda b, h, i: (b, i, h))
    kvspec = pl.BlockSpec((None, s, hw), lambda b, h, i: (b, 0, h))
    mspec = pl.BlockSpec((LANES, hw), lambda b, h, i: (0, h))
    rows = 2 * ATT_TQ
    return pl.pallas_call(
        _attn_kernel,
        grid=(bsz, DA_HEADS // ATT_HEADS, nq),
        in_specs=[pl.BlockSpec(memory_space=pltpu.SMEM), qspec, kvspec, kvspec, mspec, mspec,
                  _const_spec((1, LANES))],
        out_specs=qspec,
        out_shape=jax.ShapeDtypeStruct((bsz, s, D_MODEL), BF16),
        scratch_shapes=[pltpu.VMEM((ATT_HEADS, rows, ATT_TK), F32), pltpu.VMEM((ATT_HEADS, rows, ATT_TK), F32),
                        pltpu.VMEM((ATT_HEADS, rows, LANES), F32), pltpu.VMEM((ATT_HEADS, rows, LANES), F32),
                        pltpu.VMEM((ATT_HEADS, rows, LANES), F32)],
        compiler_params=_cparams(("parallel", "parallel", "arbitrary")),
    )(lam, q, k, v, km, vm, subln_w)


def _split3(a):
    hi = a.astype(BF16)
    r = a - hi.astype(F32)
    mid = r.astype(BF16)
    lo = (r - mid.astype(F32)).astype(BF16)
    return hi, mid, lo


def _ssd_kernel(x_ref, meta_ref, g_ref, b_ref, wz_ref, wxbc_ref, wdt_ref, cw_ref, cb_ref,
                dtb_ref, alog_ref, dskip_ref, nw_ref, o_ref, state_ref, cbuf_ref, y_ref):
    lt, lc = SSD_TILE, SSD_CHUNK
    t = pl.program_id(1)
    is_meta = t == 0

    @pl.when(is_meta)
    def _():
        state_ref[...] = jnp.zeros_like(state_ref)
        cbuf_ref[0:8, :] = jnp.zeros((8, SSD_CONV_DIM), F32)

    row = lax.broadcasted_iota(jnp.int32, (lt, 1), 0)
    valid = jnp.logical_or(jnp.logical_not(is_meta), row >= lt - N_META)
    x = jnp.where(is_meta, meta_ref[...], x_ref[...])
    h = _layer_norm(x, g_ref[...], b_ref[...]).astype(BF16)
    z = _dot(h, wz_ref[...])
    xbc = jnp.where(valid, _dot(h, wxbc_ref[...]), 0.0)
    dtr = _dot(h, wdt_ref[...])

    cbuf_ref[8:8 + lt, :] = xbc
    conv = (cw_ref[0:1, :] * cbuf_ref[5:5 + lt, :] + cw_ref[1:2, :] * cbuf_ref[6:6 + lt, :]
            + cw_ref[2:3, :] * cbuf_ref[7:7 + lt, :] + cw_ref[3:4, :] * xbc + cb_ref[...])
    cbuf_ref[0:8, :] = cbuf_ref[lt:lt + 8, :]
    act = jnp.where(valid, conv * _sigmoid(conv), 0.0)
    xs = act[:, :SSD_INNER]
    bm = act[:, SSD_INNER:SSD_INNER + SSD_GROUPS * SSD_STATE]
    cm = act[:, SSD_INNER + SSD_GROUPS * SSD_STATE:].astype(BF16)
    bm_t = bm.T.astype(BF16)

    dtv = dtr + dtb_ref[...]
    dt = jnp.maximum(dtv, 0.0) + jnp.log1p(jnp.exp(-jnp.abs(dtv)))
    dt = jnp.where(valid, dt, 0.0)
    a = dt * (-jnp.exp(alog_ref[...]))

    lane = lax.broadcasted_iota(jnp.int32, (1, LANES), 1)
    left = lane < SSD_HEAD_DIM
    tri_r = lax.broadcasted_iota(jnp.int32, (lc, lc), 0)
    tri_c = lax.broadcasted_iota(jnp.int32, (lc, lc), 1)
    causal = tri_c <= tri_r
    tri = jnp.where(causal, 1.0, 0.0).astype(BF16)

    for c in range(lt // lc):
        rs = slice(c * lc, (c + 1) * lc)
        hi, mid, lo = _split3(a[rs])
        acs = _dot(tri, hi) + _dot(tri, mid) + _dot(tri, lo)
        acs_t = acs.T
        dt_c = dt[rs]
        for g in range(SSD_GROUPS):
            c_g = cm[rs, g * SSD_STATE:(g + 1) * SSD_STATE]
            bt_g = bm_t[g * SSD_STATE:(g + 1) * SSD_STATE, rs]
            cb = _dot(c_g, bt_g)
            for jj in range(2):
                j = 2 * g + jj
                h0, h1 = 2 * j, 2 * j + 1
                col0, col1 = acs[:, h0:h0 + 1], acs[:, h1:h1 + 1]
                row0, row1 = acs_t[h0:h0 + 1, :], acs_t[h1:h1 + 1, :]
                l0 = jnp.exp(jnp.where(causal, col0 - row0, -jnp.inf))
                l1 = jnp.exp(jnp.where(causal, col1 - row1, -jnp.inf))
                mm = jnp.concatenate([(cb * l0).astype(BF16), (cb * l1).astype(BF16)], axis=0)
                xp = xs[rs, j * LANES:(j + 1) * LANES]
                xdt = xp * jnp.where(left, dt_c[:, h0:h0 + 1], dt_c[:, h1:h1 + 1])
                yy = _dot(mm, xdt.astype(BF16))
                y_diag = jnp.where(left, yy[:lc], yy[lc:])
                acs_p = jnp.where(left, col0, col1)
                st = state_ref[j]
                y_off = _dot(c_g, st.astype(BF16)) * jnp.exp(acs_p)
                last_p = jnp.where(left, acs[lc - 1:lc, h0:h0 + 1], acs[lc - 1:lc, h1:h1 + 1])
                xd = (xdt * jnp.exp(last_p - acs_p)).astype(BF16)
                state_ref[j] = st * jnp.exp(last_p) + _dot(bt_g, xd)
                y_ref[rs, j * LANES:(j + 1) * LANES] = (
                    y_diag + y_off + xp * dskip_ref[:, j * LANES:(j + 1) * LANES])

    gy = y_ref[...] * (z * _sigmoid(z))
    gw = SSD_INNER // SSD_GROUPS
    outs = []
    for g in range(SSD_GROUPS):
        gg = gy[:, g * gw:(g + 1) * gw]
        outs.append(gg * lax.rsqrt(jnp.mean(gg * gg, axis=-1, keepdims=True) + RMS_EPS))
    o_ref[...] = (jnp.concatenate(outs, axis=1) * nw_ref[...]).astype(BF16)


def _ssd_call(x3, meta_tile, g, b, wz, wxbc, wdt, cw, cb, dtb, alog, dskip, nw):
    bsz, s, _ = x3.shape
    lt = SSD_TILE
    nt = s // lt
    xspec = pl.BlockSpec((None, lt, D_MODEL), lambda bb, t: (bb, jnp.maximum(t - 1, 0), 0))
    return pl.pallas_call(
        _ssd_kernel,
        grid=(bsz, nt + 1),
        in_specs=[xspec, _const_spec((lt, D_MODEL)), _const_spec((1, D_MODEL)),
                  _const_spec((1, D_MODEL)), _const_spec((D_MODEL, SSD_INNER)),
                  _const_spec((D_MODEL, SSD_CONV_DIM)), _const_spec((D_MODEL, LANES)),
                  _const_spec((SSD_CONV, SSD_CONV_DIM)), _const_spec((1, SSD_CONV_DIM)),
                  _const_spec((1, LANES)), _const_spec((1, LANES)), _const_spec((1, SSD_INNER)),
                  _const_spec((1, SSD_INNER))],
        out_specs=xspec,
        out_shape=jax.ShapeDtypeStruct((bsz, s, SSD_INNER), BF16),
        scratch_shapes=[pltpu.VMEM((SSD_HEADS // 2, SSD_STATE, LANES), F32),
                        pltpu.VMEM((lt + 8, SSD_CONV_DIM), F32),
                        pltpu.VMEM((lt, SSD_INNER), F32)],
        compiler_params=_cparams(("parallel", "arbitrary")),
    )(x3, meta_tile, g, b, wz, wxbc, wdt, cw, cb, dtb, alog, dskip, nw)


def _merge_kernel(x_ref, ssd_ref, da_ref, g0_ref, b0_ref, wg_ref, bg_ref, wso_ref, wdo_ref, wo_ref,
                  g1_ref, b1_ref, wr_ref, br_ref, h1_ref, hb_ref, lp_ref, lpt_ref, tab_ref):
    tm = x_ref.shape[0]
    d = D_MODEL

    h = _layer_norm(x_ref[...], g0_ref[...], b0_ref[...])
    gates = _sigmoid(_dot(h.astype(BF16), wg_ref[...]) + bg_ref[...])
    y_ssd = _dot(ssd_ref[...], wso_ref[...])
    y_da = _dot(da_ref[...], wdo_ref[...])
    merged = gates[:, :d] * y_ssd + gates[:, d:] * y_da
    mix = _dot(merged.astype(BF16), wo_ref[...])
    h1 = _layer_norm(DEEPNORM_ALPHA * h + mix, g1_ref[...], b1_ref[...])
    h1_ref[...] = h1
    hb = h1.astype(BF16)
    hb_ref[...] = hb

    logits = (_dot_nt(wr_ref[...], hb) + br_ref[...])[:N_EXPERTS]
    e_io = lax.broadcasted_iota(jnp.int32, (N_EXPERTS, tm), 0)
    vals, sels = [], []
    for k in range(TOP_K):
        mx = jnp.max(logits, axis=0, keepdims=True)
        idx = jnp.min(jnp.where(logits == mx, e_io, N_EXPERTS), axis=0, keepdims=True)
        sel = e_io == idx
        logits = jnp.where(sel, -jnp.inf, logits)
        vals.append(mx)
        sels.append(sel)
    exps = [jnp.exp(v - vals[0]) for v in vals]
    den = exps[0] + exps[1] + exps[2] + exps[3]

    onehot = jnp.where(sels[0] | sels[1] | sels[2] | sels[3], 1.0, 0.0)
    onehot_p = jnp.concatenate([onehot, jnp.zeros((LANES - N_EXPERTS, tm), F32)], axis=0).astype(BF16)
    r_i = lax.broadcasted_iota(jnp.int32, (tm, tm), 0)
    c_i = lax.broadcasted_iota(jnp.int32, (tm, tm), 1)
    earlier = jnp.where(r_i < c_i, 1.0, 0.0).astype(BF16)
    before = _dot(onehot_p, earlier)[:N_EXPERTS]
    cnt = jnp.sum(onehot_p.astype(F32), axis=1, keepdims=True)
    cnt_al = jnp.floor((cnt + (SEG_ALIGN - 1)) * (1.0 / SEG_ALIGN)) * SEG_ALIGN
    e_r = lax.broadcasted_iota(jnp.int32, (LANES, LANES), 0)
    e_c = lax.broadcasted_iota(jnp.int32, (LANES, LANES), 1)
    lower = jnp.where(e_c < e_r, 1.0, 0.0).astype(BF16)
    start = _dot(lower, jnp.broadcast_to(cnt_al, (LANES, LANES)).astype(BF16))[:, 0:1]
    pos = before + start[:N_EXPERTS]
    rows = [jnp.sum(jnp.where(sels[k], pos, 0.0), axis=0, keepdims=True) for k in range(TOP_K)]
    rows += [exps[k] / den for k in range(TOP_K)]
    lp_t = jnp.concatenate(rows, axis=0)
    lpt_ref[...] = lp_t
    lp_ref[...] = jnp.concatenate([lp_t, jnp.zeros((LANES - 2 * TOP_K, tm), F32)], axis=0).T
    cols = jnp.where(e_c == 0, cnt, jnp.where(e_c == 1, cnt_al, jnp.where(e_c == 2, start, 0.0)))
    tab_ref[...] = cols.T[:8]


def _merge_call(x2d, ssd_n, da_n, g0, b0, wg, bg, wso, wdo, wo, g1, b1, wr, br, tm):
    n = x2d.shape[0]
    d = D_MODEL
    rowf = pl.BlockSpec((tm, d), lambda i: (i, 0))
    rowl = pl.BlockSpec((tm, LANES), lambda i: (i, 0))
    return pl.pallas_call(
        _merge_kernel,
        grid=(n // tm,),
        in_specs=[rowf, rowf, rowf, _const_spec((1, d)), _const_spec((1, d)),
                  _const_spec((d, 2 * d)), _const_spec((1, 2 * d)), _const_spec((d, d)),
                  _const_spec((d, d)), _const_spec((d, d)), _const_spec((1, d)), _const_spec((1, d)),
                  _const_spec((LANES, d)), _const_spec((LANES, 1))],
        out_specs=[rowf, rowf, rowl, pl.BlockSpec((2 * TOP_K, tm), lambda i: (0, i)),
                   pl.BlockSpec((8, LANES), lambda i: (i, 0))],
        out_shape=[jax.ShapeDtypeStruct((n, d), F32), jax.ShapeDtypeStruct((n, d), BF16),
                   jax.ShapeDtypeStruct((n, LANES), F32), jax.ShapeDtypeStruct((2 * TOP_K, n), F32),
                   jax.ShapeDtypeStruct((n // tm * 8, LANES), F32)],
        compiler_params=_cparams(("parallel",)),
    )(x2d, ssd_n, da_n, g0, b0, wg, bg, wso, wdo, wo, g1, b1, wr, br)


def _run_copy(src_ref, dst_ref, s_al, d_al, n_al, sem, wait):
    for b in reversed(range(SEG_BITS)):
        size = (1 << b) * SEG_ALIGN
        done = (n_al >> (b + 1)) << (b + 1)

        @pl.when((n_al & (1 << b)) != 0)
        def _():
            s0 = pl.multiple_of((s_al + done) * SEG_ALIGN, SEG_ALIGN)
            d0 = pl.multiple_of((d_al + done) * SEG_ALIGN, SEG_ALIGN)
            cp = pltpu.make_async_copy(src_ref.at[pl.ds(s0, size), :], dst_ref.at[pl.ds(d0, size), :], sem)
            if wait:
                cp.wait()
            else:
                cp.start()


def _hits(lp_t, k, rows, r0):
    r = lax.broadcasted_iota(jnp.int32, (rows, lp_t.shape[1]), 0) + r0
    return r == lp_t[k:k + 1, :].astype(jnp.int32)


def _sort_kernel(glob_t, loc_t, len_t, fdst_t, flen_t, hb_ref, lpt_ref, xs_ref, buf_ref, zero_ref, sems):
    d = D_MODEL
    i = pl.program_id(0)
    last = pl.num_programs(0) - 1

    def sort_into(slot):
        lp_t = lpt_ref[...]
        hb = hb_ref[...]
        for c in range(SORT_ROWS // SORT_CHUNK):
            r0 = c * SORT_CHUNK
            hits = [_hits(lp_t, k, SORT_CHUNK, r0) for k in range(TOP_K)]
            sel = jnp.where(hits[0] | hits[1] | hits[2] | hits[3], 1.0, 0.0).astype(BF16)
            xsort = _dot(sel, hb)
            bits = pltpu.bitcast(xsort, jnp.uint32)
            buf_ref[slot, r0:r0 + SORT_CHUNK, :d // 2] = bits[:, :d // 2] | (bits[:, d // 2:] >> 16)
            g = jnp.where(hits[0], lp_t[TOP_K:TOP_K + 1, :], 0.0)
            for k in range(1, TOP_K):
                g = g + jnp.where(hits[k], lp_t[TOP_K + k:TOP_K + k + 1, :], 0.0)
            gsum = jnp.sum(g, axis=1, keepdims=True)
            buf_ref[slot, r0:r0 + SORT_CHUNK, d // 2:] = pltpu.bitcast(
                jnp.broadcast_to(gsum, (SORT_CHUNK, LANES)), jnp.uint32)

    def runs(tile, slot, wait):
        def one(e, carry):
            s = tile * N_EXPERTS + e
            _run_copy(buf_ref.at[slot], xs_ref, loc_t[s], glob_t[s], len_t[s], sems.at[slot], wait)
            return carry

        lax.fori_loop(0, N_EXPERTS, one, 0)

    def fills(wait):
        def one(k, carry):
            _run_copy(zero_ref, xs_ref, 0, fdst_t[k], flen_t[k], sems.at[2], wait)
            return carry

        lax.fori_loop(0, fdst_t.shape[0], one, 0)

    for slot in range(2):
        @pl.when(i % 2 == slot)
        def _():
            sort_into(slot)
            runs(i, slot, False)

            @pl.when(i > 0)
            def _():
                runs(i - 1, 1 - slot, True)

            @pl.when(i == last)
            def _():
                zero_ref[...] = jnp.zeros_like(zero_ref)
                fills(False)
                runs(i, slot, True)
                fills(True)


def _sort_call(glob_t, loc_t, len_t, fdst_t, flen_t, hb, lpt, tm, out_rows):
    n, d = hb.shape
    grid_spec = pltpu.PrefetchScalarGridSpec(
        num_scalar_prefetch=5,
        grid=(n // tm,),
        in_specs=[pl.BlockSpec((tm, d), lambda i, *_: (i, 0)),
                  pl.BlockSpec((2 * TOP_K, tm), lambda i, *_: (0, i))],
        out_specs=pl.BlockSpec(memory_space=pl.ANY),
        scratch_shapes=[pltpu.VMEM((2, SORT_ROWS, XS_WIDTH), jnp.uint32),
                        pltpu.VMEM((EXPERT_TILE, XS_WIDTH), jnp.uint32),
                        pltpu.SemaphoreType.DMA((3,))],
    )
    return pl.pallas_call(
        _sort_kernel,
        grid_spec=grid_spec,
        out_shape=jax.ShapeDtypeStruct((out_rows, XS_WIDTH), jnp.uint32),
        compiler_params=_cparams(("arbitrary",)),
    )(glob_t, loc_t, len_t, fdst_t, flen_t, hb, lpt)


def _expert_kernel(tile_ref, exp_ref, lo_ref, hi_ref, xs_ref, wgu_ref, bgu_ref, wd_ref, bd_ref, ys_ref,
                   wgu_bf_ref, wd_bf_ref):
    i = pl.program_id(0)
    lo = lo_ref[i]
    hi = hi_ref[i]
    tm = xs_ref.shape[0]
    d = D_MODEL

    @pl.when(jnp.logical_or(i == 0, exp_ref[i] != exp_ref[jnp.maximum(i - 1, 0)]))
    def _():
        wgu_bf_ref[...] = wgu_ref[...].astype(BF16)
        wd_bf_ref[...] = wd_ref[...].astype(BF16)

    @pl.when(lo > hi)
    def _():
        ys_ref[...] = jnp.zeros_like(ys_ref)

    @pl.when(lo < hi)
    def _():
        w = xs_ref[:, :d // 2]
        x_hi = pltpu.bitcast(w & jnp.uint32(0xFFFF0000), F32).astype(BF16)
        x_lo = pltpu.bitcast(w << 16, F32).astype(BF16)
        x = jnp.concatenate([x_hi, x_lo], axis=1)
        hid = _dot(x, wgu_bf_ref[...]) + bgu_ref[...]
        gate = jnp.minimum(hid[:, :D_FF], SWIGLU_LIMIT)
        up = jnp.clip(hid[:, D_FF:], -SWIGLU_LIMIT, SWIGLU_LIMIT)
        act = gate * _sigmoid(SWIGLU_ALPHA * gate) * (up + 1.0)
        y = _dot(act.astype(BF16), wd_bf_ref[...]) + bd_ref[...]
        rgate = pltpu.bitcast(xs_ref[:, d // 2:], F32)
        y = y * jnp.concatenate([rgate] * (d // LANES), axis=1)
        row = lax.broadcasted_iota(jnp.int32, (tm, 1), 0)
        mine = jnp.logical_and(row >= lo, row < hi)

        @pl.when(lo == 0)
        def _():
            ys_ref[...] = jnp.where(mine, y, 0.0)

        @pl.when(lo > 0)
        def _():
            ys_ref[...] = jnp.where(mine, y, ys_ref[...])


def _expert_call(item_tile, item_exp, item_lo, item_hi, xs, wgu, bgu, wd, bd):
    m, w = xs.shape
    tm = EXPERT_TILE
    n_items = item_tile.shape[0]
    grid_spec = pltpu.PrefetchScalarGridSpec(
        num_scalar_prefetch=4,
        grid=(n_items,),
        in_specs=[pl.BlockSpec((tm, w), lambda i, t, e, lo, hi: (t[i], 0)),
                  pl.BlockSpec((None, D_MODEL, 2 * D_FF), lambda i, t, e, lo, hi: (e[i], 0, 0)),
                  pl.BlockSpec((None, 1, 2 * D_FF), lambda i, t, e, lo, hi: (e[i], 0, 0)),
                  pl.BlockSpec((None, D_FF, D_MODEL), lambda i, t, e, lo, hi: (e[i], 0, 0)),
                  pl.BlockSpec((None, 1, D_MODEL), lambda i, t, e, lo, hi: (e[i], 0, 0))],
        out_specs=pl.BlockSpec((tm, D_MODEL), lambda i, t, e, lo, hi: (t[i], 0)),
        scratch_shapes=[pltpu.VMEM((D_MODEL, 2 * D_FF), BF16), pltpu.VMEM((D_FF, D_MODEL), BF16)],
    )
    return pl.pallas_call(
        _expert_kernel,
        grid_spec=grid_spec,
        out_shape=jax.ShapeDtypeStruct((m, D_MODEL), F32),
        compiler_params=_cparams(("arbitrary",)),
    )(item_tile, item_exp, item_lo, item_hi, xs, wgu, bgu, wd, bd)


def _combine_kernel(glob_t, loc_t, len_t, ys_ref, lp_ref, h1_ref, g2_ref, b2_ref, o_ref, ybuf_ref, sems):
    tm = h1_ref.shape[0]
    i = pl.program_id(0)
    n_steps = pl.num_programs(0)

    def fetch(tile, slot, wait):
        if not wait:
            ybuf_ref[slot, TOP_K * tm:, :] = jnp.zeros((SORT_ROWS - TOP_K * tm, D_MODEL), F32)

        def one(e, carry):
            s = tile * N_EXPERTS + e
            _run_copy(ys_ref, ybuf_ref.at[slot], glob_t[s], loc_t[s], len_t[s], sems.at[slot], wait)
            return carry

        lax.fori_loop(0, N_EXPERTS, one, 0)

    def reduce_from(slot):
        lp = lp_ref[...]
        ffn = jnp.zeros((tm, D_MODEL), F32)
        for c in range(SORT_ROWS // SORT_CHUNK):
            r0 = c * SORT_CHUNK
            r = lax.broadcasted_iota(jnp.int32, (tm, SORT_CHUNK), 1) + r0
            sel = jnp.where(r == lp[:, 0:1].astype(jnp.int32), 1.0, 0.0)
            for k in range(1, TOP_K):
                sel = sel + jnp.where(r == lp[:, k:k + 1].astype(jnp.int32), 1.0, 0.0)
            sel = sel.astype(BF16)
            y = ybuf_ref[slot, r0:r0 + SORT_CHUNK, :]
            ffn = ffn + _dot(sel, y.astype(BF16))
        o_ref[...] = _layer_norm(DEEPNORM_ALPHA * h1_ref[...] + ffn, g2_ref[...], b2_ref[...])

    @pl.when(i == 0)
    def _():
        fetch(0, 0, False)

    for slot in range(2):
        @pl.when(i % 2 == slot)
        def _():
            @pl.when(i + 1 < n_steps)
            def _():
                fetch(i + 1, 1 - slot, False)

            fetch(i, slot, True)
            reduce_from(slot)


def _combine_call(glob_t, loc_t, len_t, ys, lp, h1, g2, b2, tm):
    n, d = h1.shape
    rowf = pl.BlockSpec((tm, d), lambda i, *_: (i, 0))
    grid_spec = pltpu.PrefetchScalarGridSpec(
        num_scalar_prefetch=3,
        grid=(n // tm,),
        in_specs=[pl.BlockSpec(memory_space=pl.ANY), pl.BlockSpec((tm, LANES), lambda i, *_: (i, 0)),
                  rowf, pl.BlockSpec((1, d), lambda i, *_: (0, 0)), pl.BlockSpec((1, d), lambda i, *_: (0, 0))],
        out_specs=rowf,
        scratch_shapes=[pltpu.VMEM((2, SORT_ROWS, d), F32), pltpu.SemaphoreType.DMA((2,))],
    )
    return pl.pallas_call(
        _combine_kernel,
        grid_spec=grid_spec,
        out_shape=jax.ShapeDtypeStruct((n, d), F32),
        compiler_params=_cparams(("arbitrary",)),
    )(glob_t, loc_t, len_t, ys, lp, h1, g2, b2)


def _work_items(counts, m):
    tm = EXPERT_TILE
    n_tiles = m // tm
    max_items = n_tiles + N_EXPERTS - 1
    grp_end = jnp.cumsum(counts)
    grp_start = grp_end - counts
    first_tile = grp_start // tm
    last_tile = (grp_end - 1) // tm
    n_e = jnp.where(counts > 0, last_tile - first_tile + 1, 0)
    item_end = jnp.cumsum(n_e)
    item_start = item_end - n_e
    total = item_end[-1]
    i = jnp.arange(max_items, dtype=jnp.int32)
    valid = i < total
    ic = jnp.minimum(i, total - 1)
    e = jnp.minimum(jnp.sum(item_end[None, :] <= ic[:, None], axis=1), N_EXPERTS - 1).astype(jnp.int32)
    tile = first_tile[e] + (ic - item_start[e])
    lo = jnp.maximum(grp_start[e], tile * tm) - tile * tm
    hi = jnp.minimum(grp_end[e], (tile + 1) * tm) - tile * tm
    used_tiles = (grp_end[-1] + tm - 1) // tm
    fill_tile = used_tiles + (i - total)
    is_fill = jnp.logical_and(jnp.logical_not(valid), fill_tile < n_tiles)
    tile = jnp.where(valid, tile, jnp.where(is_fill, fill_tile, n_tiles - 1))
    lo = jnp.where(valid, lo, jnp.where(is_fill, 1, 0))
    hi = jnp.where(valid, hi, 0)
    return tile.astype(jnp.int32), e, lo.astype(jnp.int32), hi.astype(jnp.int32), grp_start


def _segment_tables(tab, n_tiles):
    al = SEG_ALIGN
    t3 = tab.reshape(n_tiles, 8, LANES)
    cnt_al = t3[:, 1, :N_EXPERTS].astype(jnp.int32)
    start = t3[:, 2, :N_EXPERTS].astype(jnp.int32)
    grp = jnp.sum(cnt_al, axis=0)
    grp_start = jnp.cumsum(grp) - grp
    ahead = jnp.cumsum(cnt_al, axis=0) - cnt_al
    local = start.reshape(-1) // al
    glob = (grp_start[None, :] + ahead).reshape(-1) // al
    length = cnt_al.reshape(-1) // al
    return grp, local, glob, length, jnp.sum(grp)


def kernel(x, meta_tokens, ln_in_g, ln_in_b, w_in, b_gate, conv_w, conv_b, dt_bias, a_log, d_skip, ssd_norm_w, w_ssd_out, lam_q1, lam_k1, lam_q2, lam_k2, subln_w, w_da_out, w_out, ln1_g, ln1_b, w_router, b_router, w_gate_up, b_gate_up, w_down, b_down, ln2_g, ln2_b):
    bsz, s, d = x.shape
    n = bsz * s
    l = 0
    row = lambda v: v.reshape(1, -1).astype(F32)

    w = w_in[l]
    c0 = SSD_INNER
    c1 = c0 + SSD_CONV_DIM
    c2 = c1 + SSD_HEADS
    c3 = c2 + 3 * D_MODEL
    w_z = w[:, :c0].astype(BF16)
    w_xbc = w[:, c0:c1].astype(BF16)
    w_dt = jnp.pad(w[:, c1:c2], ((0, 0), (0, LANES - SSD_HEADS))).astype(BF16)
    w_qkv = w[:, c2:c3].astype(BF16)
    w_g = w[:, c3:].astype(BF16)
    g0, b0 = row(ln_in_g), row(ln_in_b)

    x2d = x.reshape(n, d)
    q, k, v = _qkv_call(x2d, g0, b0, w_qkv, ROW_TILE)
    _, km, vm = _qkv_call(meta_tokens.astype(F32), g0, b0, w_qkv, N_META)
    km = jnp.pad(km, ((0, LANES - N_META), (0, 0)))
    vm = jnp.pad(vm, ((0, LANES - N_META), (0, 0)))

    lam = (jnp.exp(jnp.sum(lam_q1[l].astype(F32) * lam_k1[l].astype(F32)))
           - jnp.exp(jnp.sum(lam_q2[l].astype(F32) * lam_k2[l].astype(F32))) + LAMBDA_INIT)
    da_n = _attn_call(lam.reshape(1, 1), q.reshape(bsz, s, d), k.reshape(bsz, s, d),
                      v.reshape(bsz, s, d), km, vm, row(subln_w[l]))

    meta_tile = jnp.pad(meta_tokens.astype(F32), ((SSD_TILE - N_META, 0), (0, 0)))
    pad_h = lambda vec: jnp.pad(row(vec), ((0, 0), (0, LANES - SSD_HEADS)))
    ssd_n = _ssd_call(x, meta_tile, g0, b0, w_z, w_xbc, w_dt, conv_w[l].astype(F32), row(conv_b[l]),
                      pad_h(dt_bias[l]), pad_h(a_log[l]),
                      row(jnp.repeat(d_skip[l].astype(F32), SSD_HEAD_DIM)), row(ssd_norm_w[l]))

    w_r = jnp.pad(w_router[l].T, ((0, LANES - N_EXPERTS), (0, 0))).astype(BF16)
    b_r = jnp.pad(b_router[l].astype(F32).reshape(-1, 1), ((0, LANES - N_EXPERTS), (0, 0)))
    h1, hb, lp, lpt, tab = _merge_call(
        x2d, ssd_n.reshape(n, d), da_n.reshape(n, d), g0, b0, w_g, row(b_gate[l]),
        w_ssd_out[l].astype(BF16), w_da_out[l].astype(BF16), w_out[l].astype(BF16),
        row(ln1_g[l]), row(ln1_b[l]), w_r, b_r, ROW_TILE)
    n_tiles = n // ROW_TILE
    m_rows = n * TOP_K + n_tiles * N_EXPERTS * SEG_ALIGN
    m_rows = -(-m_rows // EXPERT_TILE) * EXPERT_TILE
    grp, local, glob, length, total = _segment_tables(tab, n_tiles)
    item_tile, item_exp, item_lo, item_hi, _ = _work_items(grp, m_rows)

    tail_len = (-total) % EXPERT_TILE
    fill_dst = total + tail_len + jnp.arange(m_rows // EXPERT_TILE - n * TOP_K // EXPERT_TILE + 1,
                                             dtype=jnp.int32) * EXPERT_TILE
    fill_len = jnp.where(fill_dst + EXPERT_TILE <= m_rows, EXPERT_TILE, 0)
    fdst = jnp.concatenate([total[None], fill_dst]).astype(jnp.int32) // SEG_ALIGN
    flen = jnp.concatenate([tail_len[None], fill_len]).astype(jnp.int32) // SEG_ALIGN
    xs = _sort_call(glob, local, length, fdst, flen, hb, lpt, ROW_TILE, m_rows)

    ys = _expert_call(item_tile, item_exp, item_lo, item_hi, xs,
                      w_gate_up[l].astype(F32), b_gate_up[l].reshape(N_EXPERTS, 1, -1).astype(F32),
                      w_down[l].astype(F32), b_down[l].reshape(N_EXPERTS, 1, -1).astype(F32))

    out = _combine_call(glob, local, length, ys, lp, h1, row(ln2_g[l]), row(ln2_b[l]), ROW_TILE)
    return out.reshape(bsz, s, d)
```

```python
import math

import jax
import jax.numpy as jnp
from jax import lax
from jax.experimental import pallas as pl
from jax.experimental.pallas import tpu as pltpu

F32 = jnp.float32
BF16 = jnp.bfloat16

D_MODEL = 1024
N_META = 16
STREAM_CHUNK = 64
SSD_HEADS = 16
SSD_HEAD_DIM = 64
SSD_INNER = 1024
SSD_GROUPS = 4
SSD_STATE = 128
SSD_CONV = 4
SSD_CONV_DIM = 2048
DA_HEADS = 8
DA_HEAD_DIM = 64
N_EXPERTS = 32
TOP_K = 4
D_FF = 1024
SWIGLU_LIMIT = 7.0
SWIGLU_ALPHA = 1.702
DEPTH = 1
DEEPNORM_ALPHA = (2.0 * DEPTH) ** 0.25
LN_EPS = 1e-5
RMS_EPS = 1e-6
LAMBDA_INIT = 0.8 - 0.6 * math.exp(-0.3 * 0)
LOG2_E = math.log2(math.e)

LANES = 128
VMEM_LIMIT = 56 * 1024 * 1024

ROW_TILE = 512
ATT_TQ = 512
ATT_TK = 512
ATT_HEADS = 4
SSD_TILE = 256
SSD_CHUNK = 128
EXPERT_TILE = 512
SEG_ALIGN = 8
SEG_BITS = (ROW_TILE // SEG_ALIGN).bit_length()
SORT_ROWS = ROW_TILE * TOP_K + N_EXPERTS * SEG_ALIGN
SORT_CHUNK = SORT_ROWS // 3
XS_WIDTH = D_MODEL // 2 + LANES


def _cparams(sem):
    return pltpu.CompilerParams(dimension_semantics=sem, vmem_limit_bytes=VMEM_LIMIT)


def _const_spec(shape):
    nd = len(shape)
    return pl.BlockSpec(shape, lambda *a: (0,) * nd)


def _layer_norm(x, g, b):
    mu = jnp.mean(x, axis=-1, keepdims=True)
    xc = x - mu
    var = jnp.mean(xc * xc, axis=-1, keepdims=True)
    return xc * lax.rsqrt(var + LN_EPS) * g + b


def _sigmoid(x):
    return 1.0 / (1.0 + jnp.exp(-x))


def _dot(a, b):
    return jnp.dot(a, b, preferred_element_type=F32)


def _dot_nt(a, b):
    return lax.dot_general(a, b, (((1,), (1,)), ((), ())), preferred_element_type=F32)


def _qkv_kernel(x_ref, g_ref, b_ref, w_ref, q_ref, k_ref, v_ref):
    h = _layer_norm(x_ref[...], g_ref[...], b_ref[...]).astype(BF16)
    acc = _dot(h, w_ref[...])
    d = D_MODEL
    q_ref[...] = (acc[:, :d] * (DA_HEAD_DIM ** -0.5 * LOG2_E)).astype(BF16)
    k_ref[...] = acc[:, d:2 * d].astype(BF16)
    v_ref[...] = acc[:, 2 * d:].astype(BF16)


def _qkv_call(x2d, g, b, w_qkv, tm):
    n = x2d.shape[0]
    out = jax.ShapeDtypeStruct((n, D_MODEL), BF16)
    row = pl.BlockSpec((tm, D_MODEL), lambda i: (i, 0))
    return pl.pallas_call(
        _qkv_kernel,
        grid=(n // tm,),
        in_specs=[row, _const_spec((1, D_MODEL)), _const_spec((1, D_MODEL)),
                  _const_spec((D_MODEL, 3 * D_MODEL))],
        out_specs=[row, row, row],
        out_shape=[out, out, out],
        compiler_params=_cparams(("parallel",)),
    )(x2d, g, b, w_qkv)


def _attn_kernel(lam_ref, q_ref, k_ref, v_ref, km_ref, vm_ref, sw_ref, o_ref,
                 sa_ref, sb_ref, m_ref, l_ref, acc_ref):
    tq, tk = ATT_TQ, ATT_TK
    i = pl.program_id(2)
    lane = lax.broadcasted_iota(jnp.int32, (1, LANES), 1)
    heads = range(ATT_HEADS)
    hs = [slice(h * LANES, (h + 1) * LANES) for h in heads]

    def stacked_q(h):
        q = q_ref[:, hs[h]]
        zero = jnp.zeros_like(q)
        return jnp.concatenate([jnp.where(lane < DA_HEAD_DIM, q, zero),
                                jnp.where(lane >= DA_HEAD_DIM, q, zero)], axis=0)

    q2 = [stacked_q(h) for h in heads]

    def scores(h, j):
        off = pl.multiple_of(j * tk, tk)
        return _dot_nt(q2[h], k_ref[pl.ds(off, tk), hs[h]])

    def absorb(h, s, j):
        m_old = m_ref[h]
        m_new = jnp.maximum(m_old, jnp.max(s, axis=1, keepdims=True))
        alpha = jnp.exp2(m_old - m_new)
        p = jnp.exp2(s - jnp.concatenate([m_new] * (tk // LANES), axis=1))
        psum = p[:, :LANES]
        for c in range(1, tk // LANES):
            psum = psum + p[:, c * LANES:(c + 1) * LANES]
        l_ref[h] = alpha * l_ref[h] + psum
        off = pl.multiple_of(j * tk, tk)
        acc_ref[h] = alpha * acc_ref[h] + _dot(p.astype(BF16), v_ref[pl.ds(off, tk), hs[h]])
        m_ref[h] = m_new

    for h in heads:
        sa_ref[h] = scores(h, 0)

    meta_ok = lax.broadcasted_iota(jnp.int32, (1, LANES), 1) < N_META
    for h in heads:
        sm = jnp.where(meta_ok, _dot_nt(q2[h], km_ref[:, hs[h]]), -jnp.inf)
        m0 = jnp.broadcast_to(jnp.max(sm, axis=1, keepdims=True), sm.shape)
        p0 = jnp.exp2(sm - m0)
        m_ref[h] = m0
        l_ref[h] = p0
        acc_ref[h] = _dot(p0.astype(BF16), vm_ref[:, hs[h]])

    def body(j, carry):
        @pl.when(j % 2 == 0)
        def _():
            nxt = [scores(h, j + 1) for h in heads]
            for h in heads:
                absorb(h, sa_ref[h], j)
            for h in heads:
                sb_ref[h] = nxt[h]

        @pl.when(j % 2 == 1)
        def _():
            nxt = [scores(h, j + 1) for h in heads]
            for h in heads:
                absorb(h, sb_ref[h], j)
            for h in heads:
                sa_ref[h] = nxt[h]

        return carry

    lax.fori_loop(0, i, body, 0)

    qc = (lax.broadcasted_iota(jnp.int32, (2 * tq, tk), 0) % tq) // STREAM_CHUNK
    kc = lax.broadcasted_iota(jnp.int32, (2 * tq, tk), 1) // STREAM_CHUNK
    vis = kc <= qc

    @pl.when(i % 2 == 0)
    def _():
        for h in heads:
            absorb(h, jnp.where(vis, sa_ref[h], -jnp.inf), i)

    @pl.when(i % 2 == 1)
    def _():
        for h in heads:
            absorb(h, jnp.where(vis, sb_ref[h], -jnp.inf), i)

    for h in heads:
        a = acc_ref[h] / jnp.sum(l_ref[h], axis=1, keepdims=True)
        o = a[:tq] - lam_ref[0, 0] * a[tq:]
        o = o * lax.rsqrt(jnp.mean(o * o, axis=-1, keepdims=True) + RMS_EPS) * sw_ref[...]
        o_ref[:, hs[h]] = (o * (1.0 - LAMBDA_INIT)).astype(BF16)


def _attn_call(lam, q, k, v, km, vm, subln_w):
    bsz, s, _ = q.shape
    nq = s // ATT_TQ
    hw = ATT_HEADS * LANES
    qspec = pl.BlockSpec((None, ATT_TQ, hw), lambda b, h, i: (b, i, h))
    kvspec = pl.BlockSpec((None, s, hw), lambda b, h, i: (b, 0, h))
    mspec = pl.BlockSpec((LANES, hw), lambda b, h, i: (0, h))
    rows = 2 * ATT_TQ
    return pl.pallas_call(
        _attn_kernel,
        grid=(bsz, DA_HEADS // ATT_HEADS, nq),
        in_specs=[pl.BlockSpec(memory_space=pltpu.SMEM), qspec, kvspec, kvspec, mspec, mspec,
                  _const_spec((1, LANES))],
        out_specs=qspec,
        out_shape=jax.ShapeDtypeStruct((bsz, s, D_MODEL), BF16),
        scratch_shapes=[pltpu.VMEM((ATT_HEADS, rows, ATT_TK), F32), pltpu.VMEM((ATT_HEADS, rows, ATT_TK), F32),
                        pltpu.VMEM((ATT_HEADS, rows, LANES), F32), pltpu.VMEM((ATT_HEADS, rows, LANES), F32),
                        pltpu.VMEM((ATT_HEADS, rows, LANES), F32)],
        compiler_params=_cparams(("parallel", "parallel", "arbitrary")),
    )(lam, q, k, v, km, vm, subln_w)


def _split3(a):
    hi = a.astype(BF16)
    r = a - hi.astype(F32)
    mid = r.astype(BF16)
    lo = (r - mid.astype(F32)).astype(BF16)
    return hi, mid, lo


def _ssd_kernel(x_ref, meta_ref, g_ref, b_ref, wz_ref, wxbc_ref, wdt_ref, cw_ref, cb_ref,
                dtb_ref, alog_ref, dskip_ref, nw_ref, o_ref, state_ref, cbuf_ref, y_ref):
    lt, lc = SSD_TILE, SSD_CHUNK
    t = pl.program_id(1)
    is_meta = t == 0

    @pl.when(is_meta)
    def _():
        state_ref[...] = jnp.zeros_like(state_ref)
        cbuf_ref[0:8, :] = jnp.zeros((8, SSD_CONV_DIM), F32)

    row = lax.broadcasted_iota(jnp.int32, (lt, 1), 0)
    valid = jnp.logical_or(jnp.logical_not(is_meta), row >= lt - N_META)
    x = jnp.where(is_meta, meta_ref[...], x_ref[...])
    h = _layer_norm(x, g_ref[...], b_ref[...]).astype(BF16)
    z = _dot(h, wz_ref[...])
    xbc = jnp.where(valid, _dot(h, wxbc_ref[...]), 0.0)
    dtr = _dot(h, wdt_ref[...])

    cbuf_ref[8:8 + lt, :] = xbc
    conv = (cw_ref[0:1, :] * cbuf_ref[5:5 + lt, :] + cw_ref[1:2, :] * cbuf_ref[6:6 + lt, :]
            + cw_ref[2:3, :] * cbuf_ref[7:7 + lt, :] + cw_ref[3:4, :] * xbc + cb_ref[...])
    cbuf_ref[0:8, :] = cbuf_ref[lt:lt + 8, :]
    act = jnp.where(valid, conv * _sigmoid(conv), 0.0)
    xs = act[:, :SSD_INNER]
    bm = act[:, SSD_INNER:SSD_INNER + SSD_GROUPS * SSD_STATE]
    cm = act[:, SSD_INNER + SSD_GROUPS * SSD_STATE:].astype(BF16)
    bm_t = bm.T.astype(BF16)

    dtv = dtr + dtb_ref[...]
    dt = jnp.maximum(dtv, 0.0) + jnp.log1p(jnp.exp(-jnp.abs(dtv)))
    dt = jnp.where(valid, dt, 0.0)
    a = dt * (-jnp.exp(alog_ref[...]))

    lane = lax.broadcasted_iota(jnp.int32, (1, LANES), 1)
    left = lane < SSD_HEAD_DIM
    tri_r = lax.broadcasted_iota(jnp.int32, (lc, lc), 0)
    tri_c = lax.broadcasted_iota(jnp.int32, (lc, lc), 1)
    causal = tri_c <= tri_r
    tri = jnp.where(causal, 1.0, 0.0).astype(BF16)

    for c in range(lt // lc):
        rs = slice(c * lc, (c + 1) * lc)
        hi, mid, lo = _split3(a[rs])
        acs = _dot(tri, hi) + _dot(tri, mid) + _dot(tri, lo)
        acs_t = acs.T
        dt_c = dt[rs]
        for g in range(SSD_GROUPS):
            c_g = cm[rs, g * SSD_STATE:(g + 1) * SSD_STATE]
            bt_g = bm_t[g * SSD_STATE:(g + 1) * SSD_STATE, rs]
            cb = _dot(c_g, bt_g)
            for jj in range(2):
                j = 2 * g + jj
                h0, h1 = 2 * j, 2 * j + 1
                col0, col1 = acs[:, h0:h0 + 1], acs[:, h1:h1 + 1]
                row0, row1 = acs_t[h0:h0 + 1, :], acs_t[h1:h1 + 1, :]
                l0 = jnp.exp(jnp.where(causal, col0 - row0, -jnp.inf))
                l1 = jnp.exp(jnp.where(causal, col1 - row1, -jnp.inf))
                mm = jnp.concatenate([(cb * l0).astype(BF16), (cb * l1).astype(BF16)], axis=0)
                xp = xs[rs, j * LANES:(j + 1) * LANES]
                xdt = xp * jnp.where(left, dt_c[:, h0:h0 + 1], dt_c[:, h1:h1 + 1])
                yy = _dot(mm, xdt.astype(BF16))
                y_diag = jnp.where(left, yy[:lc], yy[lc:])
                acs_p = jnp.where(left, col0, col1)
                st = state_ref[j]
                y_off = _dot(c_g, st.astype(BF16)) * jnp.exp(acs_p)
                last_p = jnp.where(left, acs[lc - 1:lc, h0:h0 + 1], acs[lc - 1:lc, h1:h1 + 1])
                xd = (xdt * jnp.exp(last_p - acs_p)).astype(BF16)
                state_ref[j] = st * jnp.exp(last_p) + _dot(bt_g, xd)
                y_ref[rs, j * LANES:(j + 1) * LANES] = (
                    y_diag + y_off + xp * dskip_ref[:, j * LANES:(j + 1) * LANES])

    gy = y_ref[...] * (z * _sigmoid(z))
    gw = SSD_INNER // SSD_GROUPS
    outs = []
    for g in range(SSD_GROUPS):
        gg = gy[:, g * gw:(g + 1) * gw]
        outs.append(gg * lax.rsqrt(jnp.mean(gg * gg, axis=-1, keepdims=True) + RMS_EPS))
    o_ref[...] = (jnp.concatenate(outs, axis=1) * nw_ref[...]).astype(BF16)


def _ssd_call(x3, meta_tile, g, b, wz, wxbc, wdt, cw, cb, dtb, alog, dskip, nw):
    bsz, s, _ = x3.shape
    lt = SSD_TILE
    nt = s // lt
    xspec = pl.BlockSpec((None, lt, D_MODEL), lambda bb, t: (bb, jnp.maximum(t - 1, 0), 0))
    return pl.pallas_call(
        _ssd_kernel,
        grid=(bsz, nt + 1),
        in_specs=[xspec, _const_spec((lt, D_MODEL)), _const_spec((1, D_MODEL)),
                  _const_spec((1, D_MODEL)), _const_spec((D_MODEL, SSD_INNER)),
                  _const_spec((D_MODEL, SSD_CONV_DIM)), _const_spec((D_MODEL, LANES)),
                  _const_spec((SSD_CONV, SSD_CONV_DIM)), _const_spec((1, SSD_CONV_DIM)),
                  _const_spec((1, LANES)), _const_spec((1, LANES)), _const_spec((1, SSD_INNER)),
                  _const_spec((1, SSD_INNER))],
        out_specs=xspec,
        out_shape=jax.ShapeDtypeStruct((bsz, s, SSD_INNER), BF16),
        scratch_shapes=[pltpu.VMEM((SSD_HEADS // 2, SSD_STATE, LANES), F32),
                        pltpu.VMEM((lt + 8, SSD_CONV_DIM), F32),
                        pltpu.VMEM((lt, SSD_INNER), F32)],
        compiler_params=_cparams(("parallel", "arbitrary")),
    )(x3, meta_tile, g, b, wz, wxbc, wdt, cw, cb, dtb, alog, dskip, nw)


def _merge_kernel(x_ref, ssd_ref, da_ref, g0_ref, b0_ref, wg_ref, bg_ref, wso_ref, wdo_ref, wo_ref,
                  g1_ref, b1_ref, wr_ref, br_ref, h1_ref, hb_ref, lp_ref, lpt_ref, tab_ref):
    tm = x_ref.shape[0]
    d = D_MODEL

    h = _layer_norm(x_ref[...], g0_ref[...], b0_ref[...])
    gates = _sigmoid(_dot(h.astype(BF16), wg_ref[...]) + bg_ref[...])
    y_ssd = _dot(ssd_ref[...], wso_ref[...])
    y_da = _dot(da_ref[...], wdo_ref[...])
    merged = gates[:, :d] * y_ssd + gates[:, d:] * y_da
    mix = _dot(merged.astype(BF16), wo_ref[...])
    h1 = _layer_norm(DEEPNORM_ALPHA * h + mix, g1_ref[...], b1_ref[...])
    h1_ref[...] = h1
    hb = h1.astype(BF16)
    hb_ref[...] = hb

    logits = (_dot_nt(wr_ref[...], hb) + br_ref[...])[:N_EXPERTS]
    e_io = lax.broadcasted_iota(jnp.int32, (N_EXPERTS, tm), 0)
    vals, sels = [], []
    for k in range(TOP_K):
        mx = jnp.max(logits, axis=0, keepdims=True)
        idx = jnp.min(jnp.where(logits == mx, e_io, N_EXPERTS), axis=0, keepdims=True)
        sel = e_io == idx
        logits = jnp.where(sel, -jnp.inf, logits)
        vals.append(mx)
        sels.append(sel)
    exps = [jnp.exp(v - vals[0]) for v in vals]
    den = exps[0] + exps[1] + exps[2] + exps[3]

    onehot = jnp.where(sels[0] | sels[1] | sels[2] | sels[3], 1.0, 0.0)
    onehot_p = jnp.concatenate([onehot, jnp.zeros((LANES - N_EXPERTS, tm), F32)], axis=0).astype(BF16)
    r_i = lax.broadcasted_iota(jnp.int32, (tm, tm), 0)
    c_i = lax.broadcasted_iota(jnp.int32, (tm, tm), 1)
    earlier = jnp.where(r_i < c_i, 1.0, 0.0).astype(BF16)
    before = _dot(onehot_p, earlier)[:N_EXPERTS]
    cnt = jnp.sum(onehot_p.astype(F32), axis=1, keepdims=True)
    cnt_al = jnp.floor((cnt + (SEG_ALIGN - 1)) * (1.0 / SEG_ALIGN)) * SEG_ALIGN
    e_r = lax.broadcasted_iota(jnp.int32, (LANES, LANES), 0)
    e_c = lax.broadcasted_iota(jnp.int32, (LANES, LANES), 1)
    lower = jnp.where(e_c < e_r, 1.0, 0.0).astype(BF16)
    start = _dot(lower, jnp.broadcast_to(cnt_al, (LANES, LANES)).astype(BF16))[:, 0:1]
    pos = before + start[:N_EXPERTS]
    rows = [jnp.sum(jnp.where(sels[k], pos, 0.0), axis=0, keepdims=True) for k in range(TOP_K)]
    rows += [exps[k] / den for k in range(TOP_K)]
    lp_t = jnp.concatenate(rows, axis=0)
    lpt_ref[...] = lp_t
    lp_ref[...] = jnp.concatenate([lp_t, jnp.zeros((LANES - 2 * TOP_K, tm), F32)], axis=0).T
    cols = jnp.where(e_c == 0, cnt, jnp.where(e_c == 1, cnt_al, jnp.where(e_c == 2, start, 0.0)))
    tab_ref[...] = cols.T[:8]


def _merge_call(x2d, ssd_n, da_n, g0, b0, wg, bg, wso, wdo, wo, g1, b1, wr, br, tm):
    n = x2d.shape[0]
    d = D_MODEL
    rowf = pl.BlockSpec((tm, d), lambda i: (i, 0))
    rowl = pl.BlockSpec((tm, LANES), lambda i: (i, 0))
    return pl.pallas_call(
        _merge_kernel,
        grid=(n // tm,),
        in_specs=[rowf, rowf, rowf, _const_spec((1, d)), _const_spec((1, d)),
                  _const_spec((d, 2 * d)), _const_spec((1, 2 * d)), _const_spec((d, d)),
                  _const_spec((d, d)), _const_spec((d, d)), _const_spec((1, d)), _const_spec((1, d)),
                  _const_spec((LANES, d)), _const_spec((LANES, 1))],
        out_specs=[rowf, rowf, rowl, pl.BlockSpec((2 * TOP_K, tm), lambda i: (0, i)),
                   pl.BlockSpec((8, LANES), lambda i: (i, 0))],
        out_shape=[jax.ShapeDtypeStruct((n, d), F32), jax.ShapeDtypeStruct((n, d), BF16),
                   jax.ShapeDtypeStruct((n, LANES), F32), jax.ShapeDtypeStruct((2 * TOP_K, n), F32),
                   jax.ShapeDtypeStruct((n // tm * 8, LANES), F32)],
        compiler_params=_cparams(("parallel",)),
    )(x2d, ssd_n, da_n, g0, b0, wg, bg, wso, wdo, wo, g1, b1, wr, br)


def _run_copy(src_ref, dst_ref, s_al, d_al, n_al, sem, wait):
    for b in reversed(range(SEG_BITS)):
        size = (1 << b) * SEG_ALIGN
        done = (n_al >> (b + 1)) << (b + 1)

        @pl.when((n_al & (1 << b)) != 0)
        def _():
            s0 = pl.multiple_of((s_al + done) * SEG_ALIGN, SEG_ALIGN)
            d0 = pl.multiple_of((d_al + done) * SEG_ALIGN, SEG_ALIGN)
            cp = pltpu.make_async_copy(src_ref.at[pl.ds(s0, size), :], dst_ref.at[pl.ds(d0, size), :], sem)
            if wait:
                cp.wait()
            else:
                cp.start()


def _hits(lp_t, k, rows, r0):
    r = lax.broadcasted_iota(jnp.int32, (rows, lp_t.shape[1]), 0) + r0
    return r == lp_t[k:k + 1, :].astype(jnp.int32)


def _sort_kernel(glob_t, loc_t, len_t, fdst_t, flen_t, hb_ref, lpt_ref, xs_ref, buf_ref, zero_ref, sems):
    d = D_MODEL
    i = pl.program_id(0)
    last = pl.num_programs(0) - 1

    def sort_into(slot):
        lp_t = lpt_ref[...]
        hb = hb_ref[...]
        for c in range(SORT_ROWS // SORT_CHUNK):
            r0 = c * SORT_CHUNK
            hits = [_hits(lp_t, k, SORT_CHUNK, r0) for k in range(TOP_K)]
            sel = jnp.where(hits[0] | hits[1] | hits[2] | hits[3], 1.0, 0.0).astype(BF16)
            xsort = _dot(sel, hb)
            bits = pltpu.bitcast(xsort, jnp.uint32)
            buf_ref[slot, r0:r0 + SORT_CHUNK, :d // 2] = bits[:, :d // 2] | (bits[:, d // 2:] >> 16)
            g = jnp.where(hits[0], lp_t[TOP_K:TOP_K + 1, :], 0.0)
            for k in range(1, TOP_K):
                g = g + jnp.where(hits[k], lp_t[TOP_K + k:TOP_K + k + 1, :], 0.0)
            gsum = jnp.sum(g, axis=1, keepdims=True)
            buf_ref[slot, r0:r0 + SORT_CHUNK, d // 2:] = pltpu.bitcast(
                jnp.broadcast_to(gsum, (SORT_CHUNK, LANES)), jnp.uint32)

    def runs(tile, slot, wait):
        def one(e, carry):
            s = tile * N_EXPERTS + e
            _run_copy(buf_ref.at[slot], xs_ref, loc_t[s], glob_t[s], len_t[s], sems.at[slot], wait)
            return carry

        lax.fori_loop(0, N_EXPERTS, one, 0)

    def fills(wait):
        def one(k, carry):
            _run_copy(zero_ref, xs_ref, 0, fdst_t[k], flen_t[k], sems.at[2], wait)
            return carry

        lax.fori_loop(0, fdst_t.shape[0], one, 0)

    for slot in range(2):
        @pl.when(i % 2 == slot)
        def _():
            sort_into(slot)
            runs(i, slot, False)

            @pl.when(i > 0)
            def _():
                runs(i - 1, 1 - slot, True)

            @pl.when(i == last)
            def _():
                zero_ref[...] = jnp.zeros_like(zero_ref)
                fills(False)
                runs(i, slot, True)
                fills(True)


def _sort_call(glob_t, loc_t, len_t, fdst_t, flen_t, hb, lpt, tm, out_rows):
    n, d = hb.shape
    grid_spec = pltpu.PrefetchScalarGridSpec(
        num_scalar_prefetch=5,
        grid=(n // tm,),
        in_specs=[pl.BlockSpec((tm, d), lambda i, *_: (i, 0)),
                  pl.BlockSpec((2 * TOP_K, tm), lambda i, *_: (0, i))],
        out_specs=pl.BlockSpec(memory_space=pl.ANY),
        scratch_shapes=[pltpu.VMEM((2, SORT_ROWS, XS_WIDTH), jnp.uint32),
                        pltpu.VMEM((EXPERT_TILE, XS_WIDTH), jnp.uint32),
                        pltpu.SemaphoreType.DMA((3,))],
    )
    return pl.pallas_call(
        _sort_kernel,
        grid_spec=grid_spec,
        out_shape=jax.ShapeDtypeStruct((out_rows, XS_WIDTH), jnp.uint32),
        compiler_params=_cparams(("arbitrary",)),
    )(glob_t, loc_t, len_t, fdst_t, flen_t, hb, lpt)


def _expert_kernel(tile_ref, exp_ref, lo_ref, hi_ref, xs_ref, wgu_ref, bgu_ref, wd_ref, bd_ref, ys_ref,
                   wgu_bf_ref, wd_bf_ref):
    i = pl.program_id(0)
    lo = lo_ref[i]
    hi = hi_ref[i]
    tm = xs_ref.shape[0]
    d = D_MODEL

    @pl.when(jnp.logical_or(i == 0, exp_ref[i] != exp_ref[jnp.maximum(i - 1, 0)]))
    def _():
        wgu_bf_ref[...] = wgu_ref[...].astype(BF16)
        wd_bf_ref[...] = wd_ref[...].astype(BF16)

    @pl.when(lo > hi)
    def _():
        ys_ref[...] = jnp.zeros_like(ys_ref)

    @pl.when(lo < hi)
    def _():
        w = xs_ref[:, :d // 2]
        x_hi = pltpu.bitcast(w & jnp.uint32(0xFFFF0000), F32).astype(BF16)
        x_lo = pltpu.bitcast(w << 16, F32).astype(BF16)
        x = jnp.concatenate([x_hi, x_lo], axis=1)
        hid = _dot(x, wgu_bf_ref[...]) + bgu_ref[...]
        gate = jnp.minimum(hid[:, :D_FF], SWIGLU_LIMIT)
        up = jnp.clip(hid[:, D_FF:], -SWIGLU_LIMIT, SWIGLU_LIMIT)
        act = gate * _sigmoid(SWIGLU_ALPHA * gate) * (up + 1.0)
        y = _dot(act.astype(BF16), wd_bf_ref[...]) + bd_ref[...]
        rgate = pltpu.bitcast(xs_ref[:, d // 2:], F32)
        y = y * jnp.concatenate([rgate] * (d // LANES), axis=1)
        row = lax.broadcasted_iota(jnp.int32, (tm, 1), 0)
        mine = jnp.logical_and(row >= lo, row < hi)

        @pl.when(lo == 0)
        def _():
            ys_ref[...] = jnp.where(mine, y, 0.0)

        @pl.when(lo > 0)
        def _():
            ys_ref[...] = jnp.where(mine, y, ys_ref[...])


def _expert_call(item_tile, item_exp, item_lo, item_hi, xs, wgu, bgu, wd, bd):
    m, w = xs.shape
    tm = EXPERT_TILE
    n_items = item_tile.shape[0]
    grid_spec = pltpu.PrefetchScalarGridSpec(
        num_scalar_prefetch=4,
        grid=(n_items,),
        in_specs=[pl.BlockSpec((tm, w), lambda i, t, e, lo, hi: (t[i], 0)),
                  pl.BlockSpec((None, D_MODEL, 2 * D_FF), lambda i, t, e, lo, hi: (e[i], 0, 0)),
                  pl.BlockSpec((None, 1, 2 * D_FF), lambda i, t, e, lo, hi: (e[i], 0, 0)),
                  pl.BlockSpec((None, D_FF, D_MODEL), lambda i, t, e, lo, hi: (e[i], 0, 0)),
                  pl.BlockSpec((None, 1, D_MODEL), lambda i, t, e, lo, hi: (e[i], 0, 0))],
        out_specs=pl.BlockSpec((tm, D_MODEL), lambda i, t, e, lo, hi: (t[i], 0)),
        scratch_shapes=[pltpu.VMEM((D_MODEL, 2 * D_FF), BF16), pltpu.VMEM((D_FF, D_MODEL), BF16)],
    )
    return pl.pallas_call(
        _expert_kernel,
        grid_spec=grid_spec,
        out_shape=jax.ShapeDtypeStruct((m, D_MODEL), F32),
        compiler_params=_cparams(("arbitrary",)),
    )(item_tile, item_exp, item_lo, item_hi, xs, wgu, bgu, wd, bd)


def _combine_kernel(glob_t, loc_t, len_t, ys_ref, lp_ref, h1_ref, g2_ref, b2_ref, o_ref, ybuf_ref, sems):
    tm = h1_ref.shape[0]
    i = pl.program_id(0)
    n_steps = pl.num_programs(0)

    def fetch(tile, slot, wait):
        if not wait:
            ybuf_ref[slot, TOP_K * tm:, :] = jnp.zeros((SORT_ROWS - TOP_K * tm, D_MODEL), F32)

        def one(e, carry):
            s = tile * N_EXPERTS + e
            _run_copy(ys_ref, ybuf_ref.at[slot], glob_t[s], loc_t[s], len_t[s], sems.at[slot], wait)
            return carry

        lax.fori_loop(0, N_EXPERTS, one, 0)

    def reduce_from(slot):
        lp = lp_ref[...]
        ffn = jnp.zeros((tm, D_MODEL), F32)
        for c in range(SORT_ROWS // SORT_CHUNK):
            r0 = c * SORT_CHUNK
            r = lax.broadcasted_iota(jnp.int32, (tm, SORT_CHUNK), 1) + r0
            sel = jnp.where(r == lp[:, 0:1].astype(jnp.int32), 1.0, 0.0)
            for k in range(1, TOP_K):
                sel = sel + jnp.where(r == lp[:, k:k + 1].astype(jnp.int32), 1.0, 0.0)
            sel = sel.astype(BF16)
            y = ybuf_ref[slot, r0:r0 + SORT_CHUNK, :]
            ffn = ffn + _dot(sel, y.astype(BF16))
        o_ref[...] = _layer_norm(DEEPNORM_ALPHA * h1_ref[...] + ffn, g2_ref[...], b2_ref[...])

    @pl.when(i == 0)
    def _():
        fetch(0, 0, False)

    for slot in range(2):
        @pl.when(i % 2 == slot)
        def _():
            @pl.when(i + 1 < n_steps)
            def _():
                fetch(i + 1, 1 - slot, False)

            fetch(i, slot, True)
            reduce_from(slot)


def _combine_call(glob_t, loc_t, len_t, ys, lp, h1, g2, b2, tm):
    n, d = h1.shape
    rowf = pl.BlockSpec((tm, d), lambda i, *_: (i, 0))
    grid_spec = pltpu.PrefetchScalarGridSpec(
        num_scalar_prefetch=3,
        grid=(n // tm,),
        in_specs=[pl.BlockSpec(memory_space=pl.ANY), pl.BlockSpec((tm, LANES), lambda i, *_: (i, 0)),
                  rowf, pl.BlockSpec((1, d), lambda i, *_: (0, 0)), pl.BlockSpec((1, d), lambda i, *_: (0, 0))],
        out_specs=rowf,
        scratch_shapes=[pltpu.VMEM((2, SORT_ROWS, d), F32), pltpu.SemaphoreType.DMA((2,))],
    )
    return pl.pallas_call(
        _combine_kernel,
        grid_spec=grid_spec,
        out_shape=jax.ShapeDtypeStruct((n, d), F32),
        compiler_params=_cparams(("arbitrary",)),
    )(glob_t, loc_t, len_t, ys, lp, h1, g2, b2)


def _work_items(counts, m):
    tm = EXPERT_TILE
    n_tiles = m // tm
    max_items = n_tiles + N_EXPERTS - 1
    grp_end = jnp.cumsum(counts)
    grp_start = grp_end - counts
    first_tile = grp_start // tm
    last_tile = (grp_end - 1) // tm
    n_e = jnp.where(counts > 0, last_tile - first_tile + 1, 0)
    item_end = jnp.cumsum(n_e)
    item_start = item_end - n_e
    total = item_end[-1]
    i = jnp.arange(max_items, dtype=jnp.int32)
    valid = i < total
    ic = jnp.minimum(i, total - 1)
    e = jnp.minimum(jnp.sum(item_end[None, :] <= ic[:, None], axis=1), N_EXPERTS - 1).astype(jnp.int32)
    tile = first_tile[e] + (ic - item_start[e])
    lo = jnp.maximum(grp_start[e], tile * tm) - tile * tm
    hi = jnp.minimum(grp_end[e], (tile + 1) * tm) - tile * tm
    used_tiles = (grp_end[-1] + tm - 1) // tm
    fill_tile = used_tiles + (i - total)
    is_fill = jnp.logical_and(jnp.logical_not(valid), fill_tile < n_tiles)
    tile = jnp.where(valid, tile, jnp.where(is_fill, fill_tile, n_tiles - 1))
    lo = jnp.where(valid, lo, jnp.where(is_fill, 1, 0))
    hi = jnp.where(valid, hi, 0)
    return tile.astype(jnp.int32), e, lo.astype(jnp.int32), hi.astype(jnp.int32), grp_start


def _segment_tables(tab, n_tiles):
    al = SEG_ALIGN
    t3 = tab.reshape(n_tiles, 8, LANES)
    cnt_al = t3[:, 1, :N_EXPERTS].astype(jnp.int32)
    start = t3[:, 2, :N_EXPERTS].astype(jnp.int32)
    grp = jnp.sum(cnt_al, axis=0)
    grp_start = jnp.cumsum(grp) - grp
    ahead = jnp.cumsum(cnt_al, axis=0) - cnt_al
    local = start.reshape(-1) // al
    glob = (grp_start[None, :] + ahead).reshape(-1) // al
    length = cnt_al.reshape(-1) // al
    return grp, local, glob, length, jnp.sum(grp)


def kernel(x, meta_tokens, ln_in_g, ln_in_b, w_in, b_gate, conv_w, conv_b, dt_bias, a_log, d_skip, ssd_norm_w, w_ssd_out, lam_q1, lam_k1, lam_q2, lam_k2, subln_w, w_da_out, w_out, ln1_g, ln1_b, w_router, b_router, w_gate_up, b_gate_up, w_down, b_down, ln2_g, ln2_b):
    bsz, s, d = x.shape
    n = bsz * s
    l = 0
    row = lambda v: v.reshape(1, -1).astype(F32)

    w = w_in[l]
    c0 = SSD_INNER
    c1 = c0 + SSD_CONV_DIM
    c2 = c1 + SSD_HEADS
    c3 = c2 + 3 * D_MODEL
    w_z = w[:, :c0].astype(BF16)
    w_xbc = w[:, c0:c1].astype(BF16)
    w_dt = jnp.pad(w[:, c1:c2], ((0, 0), (0, LANES - SSD_HEADS))).astype(BF16)
    w_qkv = w[:, c2:c3].astype(BF16)
    w_g = w[:, c3:].astype(BF16)
    g0, b0 = row(ln_in_g), row(ln_in_b)

    x2d = x.reshape(n, d)
    q, k, v = _qkv_call(x2d, g0, b0, w_qkv, ROW_TILE)
    _, km, vm = _qkv_call(meta_tokens.astype(F32), g0, b0, w_qkv, N_META)
    km = jnp.pad(km, ((0, LANES - N_META), (0, 0)))
    vm = jnp.pad(vm, ((0, LANES - N_META), (0, 0)))

    lam = (jnp.exp(jnp.sum(lam_q1[l].astype(F32) * lam_k1[l].astype(F32)))
           - jnp.exp(jnp.sum(lam_q2[l].astype(F32) * lam_k2[l].astype(F32))) + LAMBDA_INIT)
    da_n = _attn_call(lam.reshape(1, 1), q.reshape(bsz, s, d), k.reshape(bsz, s, d),
                      v.reshape(bsz, s, d), km, vm, row(subln_w[l]))

    meta_tile = jnp.pad(meta_tokens.astype(F32), ((SSD_TILE - N_META, 0), (0, 0)))
    pad_h = lambda vec: jnp.pad(row(vec), ((0, 0), (0, LANES - SSD_HEADS)))
    ssd_n = _ssd_call(x, meta_tile, g0, b0, w_z, w_xbc, w_dt, conv_w[l].astype(F32), row(conv_b[l]),
                      pad_h(dt_bias[l]), pad_h(a_log[l]),
                      row(jnp.repeat(d_skip[l].astype(F32), SSD_HEAD_DIM)), row(ssd_norm_w[l]))

    w_r = jnp.pad(w_router[l].T, ((0, LANES - N_EXPERTS), (0, 0))).astype(BF16)
    b_r = jnp.pad(b_router[l].astype(F32).reshape(-1, 1), ((0, LANES - N_EXPERTS), (0, 0)))
    h1, hb, lp, lpt, tab = _merge_call(
        x2d, ssd_n.reshape(n, d), da_n.reshape(n, d), g0, b0, w_g, row(b_gate[l]),
        w_ssd_out[l].astype(BF16), w_da_out[l].astype(BF16), w_out[l].astype(BF16),
        row(ln1_g[l]), row(ln1_b[l]), w_r, b_r, ROW_TILE)
    n_tiles = n // ROW_TILE
    m_rows = n * TOP_K + n_tiles * N_EXPERTS * SEG_ALIGN
    m_rows = -(-m_rows // EXPERT_TILE) * EXPERT_TILE
    grp, local, glob, length, total = _segment_tables(tab, n_tiles)
    item_tile, item_exp, item_lo, item_hi, _ = _work_items(grp, m_rows)

    tail_len = (-total) % EXPERT_TILE
    fill_dst = total + tail_len + jnp.arange(m_rows // EXPERT_TILE - n * TOP_K // EXPERT_TILE + 1,
                                             dtype=jnp.int32) * EXPERT_TILE
    fill_len = jnp.where(fill_dst + EXPERT_TILE <= m_rows, EXPERT_TILE, 0)
    fdst = jnp.concatenate([total[None], fill_dst]).astype(jnp.int32) // SEG_ALIGN
    flen = jnp.concatenate([tail_len[None], fill_len]).astype(jnp.int32) // SEG_ALIGN
    xs = _sort_call(glob, local, length, fdst, flen, hb, lpt, ROW_TILE, m_rows)

    ys = _expert_call(item_tile, item_exp, item_lo, item_hi, xs,
                      w_gate_up[l].astype(F32), b_gate_up[l].reshape(N_EXPERTS, 1, -1).astype(F32),
                      w_down[l].astype(F32), b_down[l].reshape(N_EXPERTS, 1, -1).astype(F32))

    out = _combine_call(glob, local, length, ys, lp, h1, row(ln2_g[l]), row(ln2_b[l]), ROW_TILE)
    return out.reshape(bsz, s, d)
```

```python
import math

import jax
import jax.numpy as jnp
from jax import lax
from jax.experimental import pallas as pl
from jax.experimental.pallas import tpu as pltpu

F32 = jnp.float32
BF16 = jnp.bfloat16

D_MODEL = 1024
N_META = 16
STREAM_CHUNK = 64
SSD_HEADS = 16
SSD_HEAD_DIM = 64
SSD_INNER = 1024
SSD_GROUPS = 4
SSD_STATE = 128
SSD_CONV = 4
SSD_CONV_DIM = 2048
DA_HEADS = 8
DA_HEAD_DIM = 64
N_EXPERTS = 32
TOP_K = 4
D_FF = 1024
SWIGLU_LIMIT = 7.0
SWIGLU_ALPHA = 1.702
DEPTH = 1
DEEPNORM_ALPHA = (2.0 * DEPTH) ** 0.25
LN_EPS = 1e-5
RMS_EPS = 1e-6
LAMBDA_INIT = 0.8 - 0.6 * math.exp(-0.3 * 0)
LOG2_E = math.log2(math.e)

LANES = 128
VMEM_LIMIT = 56 * 1024 * 1024

ROW_TILE = 512
ATT_TQ = 512
ATT_TK = 512
ATT_HEADS = 4
SSD_TILE = 256
SSD_CHUNK = 128
EXPERT_TILE = 512
SEG_ALIGN = 8
SORT_ROWS = ROW_TILE * TOP_K + N_EXPERTS * SEG_ALIGN
SORT_CHUNK = SORT_ROWS // 3
XS_WIDTH = D_MODEL // 2 + LANES


def _cparams(sem):
    return pltpu.CompilerParams(dimension_semantics=sem, vmem_limit_bytes=VMEM_LIMIT)


def _const_spec(shape):
    nd = len(shape)
    return pl.BlockSpec(shape, lambda *a: (0,) * nd)


def _layer_norm(x, g, b):
    mu = jnp.mean(x, axis=-1, keepdims=True)
    xc = x - mu
    var = jnp.mean(xc * xc, axis=-1, keepdims=True)
    return xc * lax.rsqrt(var + LN_EPS) * g + b


def _sigmoid(x):
    return 1.0 / (1.0 + jnp.exp(-x))


def _dot(a, b):
    return jnp.dot(a, b, preferred_element_type=F32)


def _dot_nt(a, b):
    return lax.dot_general(a, b, (((1,), (1,)), ((), ())), preferred_element_type=F32)


def _qkv_kernel(x_ref, g_ref, b_ref, w_ref, q_ref, k_ref, v_ref):
    h = _layer_norm(x_ref[...], g_ref[...], b_ref[...]).astype(BF16)
    acc = _dot(h, w_ref[...])
    d = D_MODEL
    q_ref[...] = (acc[:, :d] * (DA_HEAD_DIM ** -0.5 * LOG2_E)).astype(BF16)
    k_ref[...] = acc[:, d:2 * d].astype(BF16)
    v_ref[...] = acc[:, 2 * d:].astype(BF16)


def _qkv_call(x2d, g, b, w_qkv, tm):
    n = x2d.shape[0]
    out = jax.ShapeDtypeStruct((n, D_MODEL), BF16)
    row = pl.BlockSpec((tm, D_MODEL), lambda i: (i, 0))
    return pl.pallas_call(
        _qkv_kernel,
        grid=(n // tm,),
        in_specs=[row, _const_spec((1, D_MODEL)), _const_spec((1, D_MODEL)),
                  _const_spec((D_MODEL, 3 * D_MODEL))],
        out_specs=[row, row, row],
        out_shape=[out, out, out],
        compiler_params=_cparams(("parallel",)),
    )(x2d, g, b, w_qkv)


def _attn_kernel(lam_ref, q_ref, k_ref, v_ref, km_ref, vm_ref, sw_ref, o_ref,
                 sa_ref, sb_ref, m_ref, l_ref, acc_ref):
    tq, tk = ATT_TQ, ATT_TK
    i = pl.program_id(2)
    lane = lax.broadcasted_iota(jnp.int32, (1, LANES), 1)
    heads = range(ATT_HEADS)
    hs = [slice(h * LANES, (h + 1) * LANES) for h in heads]

    def stacked_q(h):
        q = q_ref[:, hs[h]]
        zero = jnp.zeros_like(q)
        return jnp.concatenate([jnp.where(lane < DA_HEAD_DIM, q, zero),
                                jnp.where(lane >= DA_HEAD_DIM, q, zero)], axis=0)

    q2 = [stacked_q(h) for h in heads]

    def scores(h, j):
        off = pl.multiple_of(j * tk, tk)
        return _dot_nt(q2[h], k_ref[pl.ds(off, tk), hs[h]])

    def absorb(h, s, j):
        m_old = m_ref[h]
        m_new = jnp.maximum(m_old, jnp.max(s, axis=1, keepdims=True))
        alpha = jnp.exp2(m_old - m_new)
        p = jnp.exp2(s - jnp.concatenate([m_new] * (tk // LANES), axis=1))
        psum = p[:, :LANES]
        for c in range(1, tk // LANES):
            psum = psum + p[:, c * LANES:(c + 1) * LANES]
        l_ref[h] = alpha * l_ref[h] + psum
        off = pl.multiple_of(j * tk, tk)
        acc_ref[h] = alpha * acc_ref[h] + _dot(p.astype(BF16), v_ref[pl.ds(off, tk), hs[h]])
        m_ref[h] = m_new

    for h in heads:
        sa_ref[h] = scores(h, 0)

    meta_ok = lax.broadcasted_iota(jnp.int32, (1, LANES), 1) < N_META
    for h in heads:
        sm = jnp.where(meta_ok, _dot_nt(q2[h], km_ref[:, hs[h]]), -jnp.inf)
        m0 = jnp.broadcast_to(jnp.max(sm, axis=1, keepdims=True), sm.shape)
        p0 = jnp.exp2(sm - m0)
        m_ref[h] = m0
        l_ref[h] = p0
        acc_ref[h] = _dot(p0.astype(BF16), vm_ref[:, hs[h]])

    def body(j, carry):
        @pl.when(j % 2 == 0)
        def _():
            nxt = [scores(h, j + 1) for h in heads]
            for h in heads:
                absorb(h, sa_ref[h], j)
            for h in heads:
                sb_ref[h] = nxt[h]

        @pl.when(j % 2 == 1)
        def _():
            nxt = [scores(h, j + 1) for h in heads]
            for h in heads:
                absorb(h, sb_ref[h], j)
            for h in heads:
                sa_ref[h] = nxt[h]

        return carry

    lax.fori_loop(0, i, body, 0)

    qc = (lax.broadcasted_iota(jnp.int32, (2 * tq, tk), 0) % tq) // STREAM_CHUNK
    kc = lax.broadcasted_iota(jnp.int32, (2 * tq, tk), 1) // STREAM_CHUNK
    vis = kc <= qc

    @pl.when(i % 2 == 0)
    def _():
        for h in heads:
            absorb(h, jnp.where(vis, sa_ref[h], -jnp.inf), i)

    @pl.when(i % 2 == 1)
    def _():
        for h in heads:
            absorb(h, jnp.where(vis, sb_ref[h], -jnp.inf), i)

    for h in heads:
        a = acc_ref[h] / jnp.sum(l_ref[h], axis=1, keepdims=True)
        o = a[:tq] - lam_ref[0, 0] * a[tq:]
        o = o * lax.rsqrt(jnp.mean(o * o, axis=-1, keepdims=True) + RMS_EPS) * sw_ref[...]
        o_ref[:, hs[h]] = (o * (1.0 - LAMBDA_INIT)).astype(BF16)


def _attn_call(lam, q, k, v, km, vm, subln_w):
    bsz, s, _ = q.shape
    nq = s // ATT_TQ
    hw = ATT_HEADS * LANES
    qspec = pl.BlockSpec((None, ATT_TQ, hw), lambda b, h, i: (b, i, h))
    kvspec = pl.BlockSpec((None, s, hw), lambda b, h, i: (b, 0, h))
    mspec = pl.BlockSpec((LANES, hw), lambda b, h, i: (0, h))
    rows = 2 * ATT_TQ
    return pl.pallas_call(
        _attn_kernel,
        grid=(bsz, DA_HEADS // ATT_HEADS, nq),
        in_specs=[pl.BlockSpec(memory_space=pltpu.SMEM), qspec, kvspec, kvspec, mspec, mspec,
                  _const_spec((1, LANES))],
        out_specs=qspec,
        out_shape=jax.ShapeDtypeStruct((bsz, s, D_MODEL), BF16),
        scratch_shapes=[pltpu.VMEM((ATT_HEADS, rows, ATT_TK), F32), pltpu.VMEM((ATT_HEADS, rows, ATT_TK), F32),
                        pltpu.VMEM((ATT_HEADS, rows, LANES), F32), pltpu.VMEM((ATT_HEADS, rows, LANES), F32),
                        pltpu.VMEM((ATT_HEADS, rows, LANES), F32)],
        compiler_params=_cparams(("parallel", "parallel", "arbitrary")),
    )(lam, q, k, v, km, vm, subln_w)


def _split3(a):
    hi = a.astype(BF16)
    r = a - hi.astype(F32)
    mid = r.astype(BF16)
    lo = (r - mid.astype(F32)).astype(BF16)
    return hi, mid, lo


def _ssd_kernel(x_ref, meta_ref, g_ref, b_ref, wz_ref, wxbc_ref, wdt_ref, cw_ref, cb_ref,
                dtb_ref, alog_ref, dskip_ref, nw_ref, o_ref, state_ref, cbuf_ref, y_ref):
    lt, lc = SSD_TILE, SSD_CHUNK
    t = pl.program_id(1)
    is_meta = t == 0

    @pl.when(is_meta)
    def _():
        state_ref[...] = jnp.zeros_like(state_ref)
        cbuf_ref[0:8, :] = jnp.zeros((8, SSD_CONV_DIM), F32)

    row = lax.broadcasted_iota(jnp.int32, (lt, 1), 0)
    valid = jnp.logical_or(jnp.logical_not(is_meta), row >= lt - N_META)
    x = jnp.where(is_meta, meta_ref[...], x_ref[...])
    h = _layer_norm(x, g_ref[...], b_ref[...]).astype(BF16)
    z = _dot(h, wz_ref[...])
    xbc = jnp.where(valid, _dot(h, wxbc_ref[...]), 0.0)
    dtr = _dot(h, wdt_ref[...])

    cbuf_ref[8:8 + lt, :] = xbc
    conv = (cw_ref[0:1, :] * cbuf_ref[5:5 + lt, :] + cw_ref[1:2, :] * cbuf_ref[6:6 + lt, :]
            + cw_ref[2:3, :] * cbuf_ref[7:7 + lt, :] + cw_ref[3:4, :] * xbc + cb_ref[...])
    cbuf_ref[0:8, :] = cbuf_ref[lt:lt + 8, :]
    act = jnp.where(valid, conv * _sigmoid(conv), 0.0)
    xs = act[:, :SSD_INNER]
    bm = act[:, SSD_INNER:SSD_INNER + SSD_GROUPS * SSD_STATE]
    cm = act[:, SSD_INNER + SSD_GROUPS * SSD_STATE:].astype(BF16)
    bm_t = bm.T.astype(BF16)

    dtv = dtr + dtb_ref[...]
    dt = jnp.maximum(dtv, 0.0) + jnp.log1p(jnp.exp(-jnp.abs(dtv)))
    dt = jnp.where(valid, dt, 0.0)
    a = dt * (-jnp.exp(alog_ref[...]))

    lane = lax.broadcasted_iota(jnp.int32, (1, LANES), 1)
    left = lane < SSD_HEAD_DIM
    tri_r = lax.broadcasted_iota(jnp.int32, (lc, lc), 0)
    tri_c = lax.broadcasted_iota(jnp.int32, (lc, lc), 1)
    causal = tri_c <= tri_r
    tri = jnp.where(causal, 1.0, 0.0).astype(BF16)

    for c in range(lt // lc):
        rs = slice(c * lc, (c + 1) * lc)
        hi, mid, lo = _split3(a[rs])
        acs = _dot(tri, hi) + _dot(tri, mid) + _dot(tri, lo)
        acs_t = acs.T
        dt_c = dt[rs]
        for g in range(SSD_GROUPS):
            c_g = cm[rs, g * SSD_STATE:(g + 1) * SSD_STATE]
            bt_g = bm_t[g * SSD_STATE:(g + 1) * SSD_STATE, rs]
            cb = _dot(c_g, bt_g)
            for jj in range(2):
                j = 2 * g + jj
                h0, h1 = 2 * j, 2 * j + 1
                col0, col1 = acs[:, h0:h0 + 1], acs[:, h1:h1 + 1]
                row0, row1 = acs_t[h0:h0 + 1, :], acs_t[h1:h1 + 1, :]
                l0 = jnp.exp(jnp.where(causal, col0 - row0, -jnp.inf))
                l1 = jnp.exp(jnp.where(causal, col1 - row1, -jnp.inf))
                mm = jnp.concatenate([(cb * l0).astype(BF16), (cb * l1).astype(BF16)], axis=0)
                xp = xs[rs, j * LANES:(j + 1) * LANES]
                xdt = xp * jnp.where(left, dt_c[:, h0:h0 + 1], dt_c[:, h1:h1 + 1])
                yy = _dot(mm, xdt.astype(BF16))
                y_diag = jnp.where(left, yy[:lc], yy[lc:])
                acs_p = jnp.where(left, col0, col1)
                st = state_ref[j]
                y_off = _dot(c_g, st.astype(BF16)) * jnp.exp(acs_p)
                last_p = jnp.where(left, acs[lc - 1:lc, h0:h0 + 1], acs[lc - 1:lc, h1:h1 + 1])
                xd = (xdt * jnp.exp(last_p - acs_p)).astype(BF16)
                state_ref[j] = st * jnp.exp(last_p) + _dot(bt_g, xd)
                y_ref[rs, j * LANES:(j + 1) * LANES] = (
                    y_diag + y_off + xp * dskip_ref[:, j * LANES:(j + 1) * LANES])

    gy = y_ref[...] * (z * _sigmoid(z))
    gw = SSD_INNER // SSD_GROUPS
    outs = []
    for g in range(SSD_GROUPS):
        gg = gy[:, g * gw:(g + 1) * gw]
        outs.append(gg * lax.rsqrt(jnp.mean(gg * gg, axis=-1, keepdims=True) + RMS_EPS))
    o_ref[...] = (jnp.concatenate(outs, axis=1) * nw_ref[...]).astype(BF16)


def _ssd_call(x3, meta_tile, g, b, wz, wxbc, wdt, cw, cb, dtb, alog, dskip, nw):
    bsz, s, _ = x3.shape
    lt = SSD_TILE
    nt = s // lt
    xspec = pl.BlockSpec((None, lt, D_MODEL), lambda bb, t: (bb, jnp.maximum(t - 1, 0), 0))
    return pl.pallas_call(
        _ssd_kernel,
        grid=(bsz, nt + 1),
        in_specs=[xspec, _const_spec((lt, D_MODEL)), _const_spec((1, D_MODEL)),
                  _const_spec((1, D_MODEL)), _const_spec((D_MODEL, SSD_INNER)),
                  _const_spec((D_MODEL, SSD_CONV_DIM)), _const_spec((D_MODEL, LANES)),
                  _const_spec((SSD_CONV, SSD_CONV_DIM)), _const_spec((1, SSD_CONV_DIM)),
                  _const_spec((1, LANES)), _const_spec((1, LANES)), _const_spec((1, SSD_INNER)),
                  _const_spec((1, SSD_INNER))],
        out_specs=xspec,
        out_shape=jax.ShapeDtypeStruct((bsz, s, SSD_INNER), BF16),
        scratch_shapes=[pltpu.VMEM((SSD_HEADS // 2, SSD_STATE, LANES), F32),
                        pltpu.VMEM((lt + 8, SSD_CONV_DIM), F32),
                        pltpu.VMEM((lt, SSD_INNER), F32)],
        compiler_params=_cparams(("parallel", "arbitrary")),
    )(x3, meta_tile, g, b, wz, wxbc, wdt, cw, cb, dtb, alog, dskip, nw)


def _merge_kernel(x_ref, ssd_ref, da_ref, g0_ref, b0_ref, wg_ref, bg_ref, wso_ref, wdo_ref, wo_ref,
                  g1_ref, b1_ref, wr_ref, br_ref, h1_ref, hb_ref, lp_ref, lpt_ref, tab_ref):
    tm = x_ref.shape[0]
    d = D_MODEL

    h = _layer_norm(x_ref[...], g0_ref[...], b0_ref[...])
    gates = _sigmoid(_dot(h.astype(BF16), wg_ref[...]) + bg_ref[...])
    y_ssd = _dot(ssd_ref[...], wso_ref[...])
    y_da = _dot(da_ref[...], wdo_ref[...])
    merged = gates[:, :d] * y_ssd + gates[:, d:] * y_da
    mix = _dot(merged.astype(BF16), wo_ref[...])
    h1 = _layer_norm(DEEPNORM_ALPHA * h + mix, g1_ref[...], b1_ref[...])
    h1_ref[...] = h1
    hb = h1.astype(BF16)
    hb_ref[...] = hb

    logits = (_dot_nt(wr_ref[...], hb) + br_ref[...])[:N_EXPERTS]
    e_io = lax.broadcasted_iota(jnp.int32, (N_EXPERTS, tm), 0)
    vals, sels = [], []
    for k in range(TOP_K):
        mx = jnp.max(logits, axis=0, keepdims=True)
        idx = jnp.min(jnp.where(logits == mx, e_io, N_EXPERTS), axis=0, keepdims=True)
        sel = e_io == idx
        logits = jnp.where(sel, -jnp.inf, logits)
        vals.append(mx)
        sels.append(sel)
    exps = [jnp.exp(v - vals[0]) for v in vals]
    den = exps[0] + exps[1] + exps[2] + exps[3]

    onehot = jnp.where(sels[0] | sels[1] | sels[2] | sels[3], 1.0, 0.0)
    onehot_p = jnp.concatenate([onehot, jnp.zeros((LANES - N_EXPERTS, tm), F32)], axis=0).astype(BF16)
    r_i = lax.broadcasted_iota(jnp.int32, (tm, tm), 0)
    c_i = lax.broadcasted_iota(jnp.int32, (tm, tm), 1)
    earlier = jnp.where(r_i < c_i, 1.0, 0.0).astype(BF16)
    before = _dot(onehot_p, earlier)[:N_EXPERTS]
    cnt = jnp.sum(onehot_p.astype(F32), axis=1, keepdims=True)
    cnt_al = jnp.floor((cnt + (SEG_ALIGN - 1)) * (1.0 / SEG_ALIGN)) * SEG_ALIGN
    e_r = lax.broadcasted_iota(jnp.int32, (LANES, LANES), 0)
    e_c = lax.broadcasted_iota(jnp.int32, (LANES, LANES), 1)
    lower = jnp.where(e_c < e_r, 1.0, 0.0).astype(BF16)
    start = _dot(lower, jnp.broadcast_to(cnt_al, (LANES, LANES)).astype(BF16))[:, 0:1]
    pos = before + start[:N_EXPERTS]
    rows = [jnp.sum(jnp.where(sels[k], pos, 0.0), axis=0, keepdims=True) for k in range(TOP_K)]
    rows += [exps[k] / den for k in range(TOP_K)]
    lp_t = jnp.concatenate(rows, axis=0)
    lpt_ref[...] = lp_t
    lp_ref[...] = jnp.concatenate([lp_t, jnp.zeros((LANES - 2 * TOP_K, tm), F32)], axis=0).T
    cols = jnp.where(e_c == 0, cnt, jnp.where(e_c == 1, cnt_al, jnp.where(e_c == 2, start, 0.0)))
    tab_ref[...] = cols.T[:8]


def _merge_call(x2d, ssd_n, da_n, g0, b0, wg, bg, wso, wdo, wo, g1, b1, wr, br, tm):
    n = x2d.shape[0]
    d = D_MODEL
    rowf = pl.BlockSpec((tm, d), lambda i: (i, 0))
    rowl = pl.BlockSpec((tm, LANES), lambda i: (i, 0))
    return pl.pallas_call(
        _merge_kernel,
        grid=(n // tm,),
        in_specs=[rowf, rowf, rowf, _const_spec((1, d)), _const_spec((1, d)),
                  _const_spec((d, 2 * d)), _const_spec((1, 2 * d)), _const_spec((d, d)),
                  _const_spec((d, d)), _const_spec((d, d)), _const_spec((1, d)), _const_spec((1, d)),
                  _const_spec((LANES, d)), _const_spec((LANES, 1))],
        out_specs=[rowf, rowf, rowl, pl.BlockSpec((2 * TOP_K, tm), lambda i: (0, i)),
                   pl.BlockSpec((8, LANES), lambda i: (i, 0))],
        out_shape=[jax.ShapeDtypeStruct((n, d), F32), jax.ShapeDtypeStruct((n, d), BF16),
                   jax.ShapeDtypeStruct((n, LANES), F32), jax.ShapeDtypeStruct((2 * TOP_K, n), F32),
                   jax.ShapeDtypeStruct((n // tm * 8, LANES), F32)],
        compiler_params=_cparams(("parallel",)),
    )(x2d, ssd_n, da_n, g0, b0, wg, bg, wso, wdo, wo, g1, b1, wr, br)


def _run_copy(src_ref, dst_ref, s_al, d_al, n_al, sem, wait):
    @pl.when(n_al > 0)
    def _():
        s0 = pl.multiple_of(s_al * SEG_ALIGN, SEG_ALIGN)
        d0 = pl.multiple_of(d_al * SEG_ALIGN, SEG_ALIGN)
        rows = pl.multiple_of(n_al * SEG_ALIGN, SEG_ALIGN)
        cp = pltpu.make_async_copy(src_ref.at[pl.ds(s0, rows), :], dst_ref.at[pl.ds(d0, rows), :], sem)
        if wait:
            cp.wait()
        else:
            cp.start()


def _hits(lp_t, k, rows, r0):
    r = lax.broadcasted_iota(jnp.int32, (rows, lp_t.shape[1]), 0) + r0
    return r == lp_t[k:k + 1, :].astype(jnp.int32)


def _sort_kernel(glob_t, loc_t, len_t, fdst_t, flen_t, hb_ref, lpt_ref, xs_ref, buf_ref, zero_ref, sems):
    d = D_MODEL
    i = pl.program_id(0)
    last = pl.num_programs(0) - 1

    def sort_into(slot):
        lp_t = lpt_ref[...]
        hb = hb_ref[...]
        for c in range(SORT_ROWS // SORT_CHUNK):
            r0 = c * SORT_CHUNK
            hits = [_hits(lp_t, k, SORT_CHUNK, r0) for k in range(TOP_K)]
            sel = jnp.where(hits[0] | hits[1] | hits[2] | hits[3], 1.0, 0.0).astype(BF16)
            xsort = _dot(sel, hb)
            bits = pltpu.bitcast(xsort, jnp.uint32)
            buf_ref[slot, r0:r0 + SORT_CHUNK, :d // 2] = bits[:, :d // 2] | (bits[:, d // 2:] >> 16)
            g = jnp.where(hits[0], lp_t[TOP_K:TOP_K + 1, :], 0.0)
            for k in range(1, TOP_K):
                g = g + jnp.where(hits[k], lp_t[TOP_K + k:TOP_K + k + 1, :], 0.0)
            gsum = jnp.sum(g, axis=1, keepdims=True)
            buf_ref[slot, r0:r0 + SORT_CHUNK, d // 2:] = pltpu.bitcast(
                jnp.broadcast_to(gsum, (SORT_CHUNK, LANES)), jnp.uint32)

    def runs(tile, slot, wait):
        def one(e, carry):
            s = tile * N_EXPERTS + e
            _run_copy(buf_ref.at[slot], xs_ref, loc_t[s], glob_t[s], len_t[s], sems.at[slot], wait)
            return carry

        lax.fori_loop(0, N_EXPERTS, one, 0)

    def fills(wait):
        def one(k, carry):
            _run_copy(zero_ref, xs_ref, 0, fdst_t[k], flen_t[k], sems.at[2], wait)
            return carry

        lax.fori_loop(0, fdst_t.shape[0], one, 0)

    for slot in range(2):
        @pl.when(i % 2 == slot)
        def _():
            sort_into(slot)
            runs(i, slot, False)

            @pl.when(i > 0)
            def _():
                runs(i - 1, 1 - slot, True)

            @pl.when(i == last)
            def _():
                zero_ref[...] = jnp.zeros_like(zero_ref)
                fills(False)
                runs(i, slot, True)
                fills(True)


def _sort_call(glob_t, loc_t, len_t, fdst_t, flen_t, hb, lpt, tm, out_rows):
    n, d = hb.shape
    grid_spec = pltpu.PrefetchScalarGridSpec(
        num_scalar_prefetch=5,
        grid=(n // tm,),
        in_specs=[pl.BlockSpec((tm, d), lambda i, *_: (i, 0)),
                  pl.BlockSpec((2 * TOP_K, tm), lambda i, *_: (0, i))],
        out_specs=pl.BlockSpec(memory_space=pl.ANY),
        scratch_shapes=[pltpu.VMEM((2, SORT_ROWS, XS_WIDTH), jnp.uint32),
                        pltpu.VMEM((EXPERT_TILE, XS_WIDTH), jnp.uint32),
                        pltpu.SemaphoreType.DMA((3,))],
    )
    return pl.pallas_call(
        _sort_kernel,
        grid_spec=grid_spec,
        out_shape=jax.ShapeDtypeStruct((out_rows, XS_WIDTH), jnp.uint32),
        compiler_params=_cparams(("arbitrary",)),
    )(glob_t, loc_t, len_t, fdst_t, flen_t, hb, lpt)


def _expert_kernel(tile_ref, exp_ref, lo_ref, hi_ref, xs_ref, wgu_ref, bgu_ref, wd_ref, bd_ref, ys_ref,
                   wgu_bf_ref, wd_bf_ref):
    i = pl.program_id(0)
    lo = lo_ref[i]
    hi = hi_ref[i]
    tm = xs_ref.shape[0]
    d = D_MODEL

    @pl.when(jnp.logical_or(i == 0, exp_ref[i] != exp_ref[jnp.maximum(i - 1, 0)]))
    def _():
        wgu_bf_ref[...] = wgu_ref[...].astype(BF16)
        wd_bf_ref[...] = wd_ref[...].astype(BF16)

    @pl.when(lo > hi)
    def _():
        ys_ref[...] = jnp.zeros_like(ys_ref)

    @pl.when(lo < hi)
    def _():
        w = xs_ref[:, :d // 2]
        x_hi = pltpu.bitcast(w & jnp.uint32(0xFFFF0000), F32).astype(BF16)
        x_lo = pltpu.bitcast(w << 16, F32).astype(BF16)
        x = jnp.concatenate([x_hi, x_lo], axis=1)
        hid = _dot(x, wgu_bf_ref[...]) + bgu_ref[...]
        gate = jnp.minimum(hid[:, :D_FF], SWIGLU_LIMIT)
        up = jnp.clip(hid[:, D_FF:], -SWIGLU_LIMIT, SWIGLU_LIMIT)
        act = gate * _sigmoid(SWIGLU_ALPHA * gate) * (up + 1.0)
        y = _dot(act.astype(BF16), wd_bf_ref[...]) + bd_ref[...]
        rgate = pltpu.bitcast(xs_ref[:, d // 2:], F32)
        y = y * jnp.concatenate([rgate] * (d // LANES), axis=1)
        row = lax.broadcasted_iota(jnp.int32, (tm, 1), 0)
        mine = jnp.logical_and(row >= lo, row < hi)

        @pl.when(lo == 0)
        def _():
            ys_ref[...] = jnp.where(mine, y, 0.0)

        @pl.when(lo > 0)
        def _():
            ys_ref[...] = jnp.where(mine, y, ys_ref[...])


def _expert_call(item_tile, item_exp, item_lo, item_hi, xs, wgu, bgu, wd, bd):
    m, w = xs.shape
    tm = EXPERT_TILE
    n_items = item_tile.shape[0]
    grid_spec = pltpu.PrefetchScalarGridSpec(
        num_scalar_prefetch=4,
        grid=(n_items,),
        in_specs=[pl.BlockSpec((tm, w), lambda i, t, e, lo, hi: (t[i], 0)),
                  pl.BlockSpec((None, D_MODEL, 2 * D_FF), lambda i, t, e, lo, hi: (e[i], 0, 0)),
                  pl.BlockSpec((None, 1, 2 * D_FF), lambda i, t, e, lo, hi: (e[i], 0, 0)),
                  pl.BlockSpec((None, D_FF, D_MODEL), lambda i, t, e, lo, hi: (e[i], 0, 0)),
                  pl.BlockSpec((None, 1, D_MODEL), lambda i, t, e, lo, hi: (e[i], 0, 0))],
        out_specs=pl.BlockSpec((tm, D_MODEL), lambda i, t, e, lo, hi: (t[i], 0)),
        scratch_shapes=[pltpu.VMEM((D_MODEL, 2 * D_FF), BF16), pltpu.VMEM((D_FF, D_MODEL), BF16)],
    )
    return pl.pallas_call(
        _expert_kernel,
        grid_spec=grid_spec,
        out_shape=jax.ShapeDtypeStruct((m, D_MODEL), F32),
        compiler_params=_cparams(("arbitrary",)),
    )(item_tile, item_exp, item_lo, item_hi, xs, wgu, bgu, wd, bd)


def _combine_kernel(glob_t, loc_t, len_t, ys_ref, lp_ref, h1_ref, g2_ref, b2_ref, o_ref, ybuf_ref, sems):
    tm = h1_ref.shape[0]
    i = pl.program_id(0)
    n_steps = pl.num_programs(0)

    def fetch(tile, slot, wait):
        if not wait:
            ybuf_ref[slot, TOP_K * tm:, :] = jnp.zeros((SORT_ROWS - TOP_K * tm, D_MODEL), F32)

        def one(e, carry):
            s = tile * N_EXPERTS + e
            _run_copy(ys_ref, ybuf_ref.at[slot], glob_t[s], loc_t[s], len_t[s], sems.at[slot], wait)
            return carry

        lax.fori_loop(0, N_EXPERTS, one, 0)

    def reduce_from(slot):
        lp = lp_ref[...]
        ffn = jnp.zeros((tm, D_MODEL), F32)
        for c in range(SORT_ROWS // SORT_CHUNK):
            r0 = c * SORT_CHUNK
            r = lax.broadcasted_iota(jnp.int32, (tm, SORT_CHUNK), 1) + r0
            sel = jnp.where(r == lp[:, 0:1].astype(jnp.int32), 1.0, 0.0)
            for k in range(1, TOP_K):
                sel = sel + jnp.where(r == lp[:, k:k + 1].astype(jnp.int32), 1.0, 0.0)
            sel = sel.astype(BF16)
            y = ybuf_ref[slot, r0:r0 + SORT_CHUNK, :]
            ffn = ffn + _dot(sel, y.astype(BF16))
        o_ref[...] = _layer_norm(DEEPNORM_ALPHA * h1_ref[...] + ffn, g2_ref[...], b2_ref[...])

    @pl.when(i == 0)
    def _():
        fetch(0, 0, False)

    for slot in range(2):
        @pl.when(i % 2 == slot)
        def _():
            @pl.when(i + 1 < n_steps)
            def _():
                fetch(i + 1, 1 - slot, False)

            fetch(i, slot, True)
            reduce_from(slot)


def _combine_call(glob_t, loc_t, len_t, ys, lp, h1, g2, b2, tm):
    n, d = h1.shape
    rowf = pl.BlockSpec((tm, d), lambda i, *_: (i, 0))
    grid_spec = pltpu.PrefetchScalarGridSpec(
        num_scalar_prefetch=3,
        grid=(n // tm,),
        in_specs=[pl.BlockSpec(memory_space=pl.ANY), pl.BlockSpec((tm, LANES), lambda i, *_: (i, 0)),
                  rowf, pl.BlockSpec((1, d), lambda i, *_: (0, 0)), pl.BlockSpec((1, d), lambda i, *_: (0, 0))],
        out_specs=rowf,
        scratch_shapes=[pltpu.VMEM((2, SORT_ROWS, d), F32), pltpu.SemaphoreType.DMA((2,))],
    )
    return pl.pallas_call(
        _combine_kernel,
        grid_spec=grid_spec,
        out_shape=jax.ShapeDtypeStruct((n, d), F32),
        compiler_params=_cparams(("arbitrary",)),
    )(glob_t, loc_t, len_t, ys, lp, h1, g2, b2)


def _work_items(counts, m):
    tm = EXPERT_TILE
    n_tiles = m // tm
    max_items = n_tiles + N_EXPERTS - 1
    grp_end = jnp.cumsum(counts)
    grp_start = grp_end - counts
    first_tile = grp_start // tm
    last_tile = (grp_end - 1) // tm
    n_e = jnp.where(counts > 0, last_tile - first_tile + 1, 0)
    item_end = jnp.cumsum(n_e)
    item_start = item_end - n_e
    total = item_end[-1]
    i = jnp.arange(max_items, dtype=jnp.int32)
    valid = i < total
    ic = jnp.minimum(i, total - 1)
    e = jnp.minimum(jnp.sum(item_end[None, :] <= ic[:, None], axis=1), N_EXPERTS - 1).astype(jnp.int32)
    tile = first_tile[e] + (ic - item_start[e])
    lo = jnp.maximum(grp_start[e], tile * tm) - tile * tm
    hi = jnp.minimum(grp_end[e], (tile + 1) * tm) - tile * tm
    used_tiles = (grp_end[-1] + tm - 1) // tm
    fill_tile = used_tiles + (i - total)
    is_fill = jnp.logical_and(jnp.logical_not(valid), fill_tile < n_tiles)
    tile = jnp.where(valid, tile, jnp.where(is_fill, fill_tile, n_tiles - 1))
    lo = jnp.where(valid, lo, jnp.where(is_fill, 1, 0))
    hi = jnp.where(valid, hi, 0)
    return tile.astype(jnp.int32), e, lo.astype(jnp.int32), hi.astype(jnp.int32), grp_start


def _segment_tables(tab, n_tiles):
    al = SEG_ALIGN
    t3 = tab.reshape(n_tiles, 8, LANES)
    cnt_al = t3[:, 1, :N_EXPERTS].astype(jnp.int32)
    start = t3[:, 2, :N_EXPERTS].astype(jnp.int32)
    grp = jnp.sum(cnt_al, axis=0)
    grp_start = jnp.cumsum(grp) - grp
    ahead = jnp.cumsum(cnt_al, axis=0) - cnt_al
    local = start.reshape(-1) // al
    glob = (grp_start[None, :] + ahead).reshape(-1) // al
    length = cnt_al.reshape(-1) // al
    return grp, local, glob, length, jnp.sum(grp)


def kernel(x, meta_tokens, ln_in_g, ln_in_b, w_in, b_gate, conv_w, conv_b, dt_bias, a_log, d_skip, ssd_norm_w, w_ssd_out, lam_q1, lam_k1, lam_q2, lam_k2, subln_w, w_da_out, w_out, ln1_g, ln1_b, w_router, b_router, w_gate_up, b_gate_up, w_down, b_down, ln2_g, ln2_b):
    bsz, s, d = x.shape
    n = bsz * s
    l = 0
    row = lambda v: v.reshape(1, -1).astype(F32)

    w = w_in[l]
    c0 = SSD_INNER
    c1 = c0 + SSD_CONV_DIM
    c2 = c1 + SSD_HEADS
    c3 = c2 + 3 * D_MODEL
    w_z = w[:, :c0].astype(BF16)
    w_xbc = w[:, c0:c1].astype(BF16)
    w_dt = jnp.pad(w[:, c1:c2], ((0, 0), (0, LANES - SSD_HEADS))).astype(BF16)
    w_qkv = w[:, c2:c3].astype(BF16)
    w_g = w[:, c3:].astype(BF16)
    g0, b0 = row(ln_in_g), row(ln_in_b)

    x2d = x.reshape(n, d)
    q, k, v = _qkv_call(x2d, g0, b0, w_qkv, ROW_TILE)
    _, km, vm = _qkv_call(meta_tokens.astype(F32), g0, b0, w_qkv, N_META)
    km = jnp.pad(km, ((0, LANES - N_META), (0, 0)))
    vm = jnp.pad(vm, ((0, LANES - N_META), (0, 0)))

    lam = (jnp.exp(jnp.sum(lam_q1[l].astype(F32) * lam_k1[l].astype(F32)))
           - jnp.exp(jnp.sum(lam_q2[l].astype(F32) * lam_k2[l].astype(F32))) + LAMBDA_INIT)
    da_n = _attn_call(lam.reshape(1, 1), q.reshape(bsz, s, d), k.reshape(bsz, s, d),
                      v.reshape(bsz, s, d), km, vm, row(subln_w[l]))

    meta_tile = jnp.pad(meta_tokens.astype(F32), ((SSD_TILE - N_META, 0), (0, 0)))
    pad_h = lambda vec: jnp.pad(row(vec), ((0, 0), (0, LANES - SSD_HEADS)))
    ssd_n = _ssd_call(x, meta_tile, g0, b0, w_z, w_xbc, w_dt, conv_w[l].astype(F32), row(conv_b[l]),
                      pad_h(dt_bias[l]), pad_h(a_log[l]),
                      row(jnp.repeat(d_skip[l].astype(F32), SSD_HEAD_DIM)), row(ssd_norm_w[l]))

    w_r = jnp.pad(w_router[l].T, ((0, LANES - N_EXPERTS), (0, 0))).astype(BF16)
    b_r = jnp.pad(b_router[l].astype(F32).reshape(-1, 1), ((0, LANES - N_EXPERTS), (0, 0)))
    h1, hb, lp, lpt, tab = _merge_call(
        x2d, ssd_n.reshape(n, d), da_n.reshape(n, d), g0, b0, w_g, row(b_gate[l]),
        w_ssd_out[l].astype(BF16), w_da_out[l].astype(BF16), w_out[l].astype(BF16),
        row(ln1_g[l]), row(ln1_b[l]), w_r, b_r, ROW_TILE)
    n_tiles = n // ROW_TILE
    m_rows = n * TOP_K + n_tiles * N_EXPERTS * SEG_ALIGN
    m_rows = -(-m_rows // EXPERT_TILE) * EXPERT_TILE
    grp, local, glob, length, total = _segment_tables(tab, n_tiles)
    item_tile, item_exp, item_lo, item_hi, _ = _work_items(grp, m_rows)

    tail_len = (-total) % EXPERT_TILE
    fill_dst = total + tail_len + jnp.arange(m_rows // EXPERT_TILE - n * TOP_K // EXPERT_TILE + 1,
                                             dtype=jnp.int32) * EXPERT_TILE
    fill_len = jnp.where(fill_dst + EXPERT_TILE <= m_rows, EXPERT_TILE, 0)
    fdst = jnp.concatenate([total[None], fill_dst]).astype(jnp.int32) // SEG_ALIGN
    flen = jnp.concatenate([tail_len[None], fill_len]).astype(jnp.int32) // SEG_ALIGN
    xs = _sort_call(glob, local, length, fdst, flen, hb, lpt, ROW_TILE, m_rows)

    ys = _expert_call(item_tile, item_exp, item_lo, item_hi, xs,
                      w_gate_up[l].astype(F32), b_gate_up[l].reshape(N_EXPERTS, 1, -1).astype(F32),
                      w_down[l].astype(F32), b_down[l].reshape(N_EXPERTS, 1, -1).astype(F32))

    out = _combine_call(glob, local, length, ys, lp, h1, row(ln2_g[l]), row(ln2_b[l]), ROW_TILE)
    return out.reshape(bsz, s, d)
```
